```python
import jax, jax.numpy as jnp
from jax import lax
import numpy as np

D_MODEL = 1024
BATCH = 1
SEQ = 16384
DEPTH = 1

D_MIX = D_MODEL
RET_HEADS = 4
RET_QK_DIM = 64
RET_V_DIM = 128
RET_QK = RET_HEADS * RET_QK_DIM
RET_WIDTH = RET_HEADS * RET_V_DIM
GDN_HEADS = 4
GDN_HEAD_DIM = 128
GDN_WIDTH = GDN_HEADS * GDN_HEAD_DIM
CONV_K = 4
CONV_CH = 3 * GDN_WIDTH
CHUNK = 128
ROPE_BASE = 10000.0
NORM_EPS = 1e-6
PEER_HEADS = 8
PEER_KEYS = 128
PEER_EXPERTS = PEER_KEYS * PEER_KEYS
PEER_QDIM = 256
PEER_HALF = PEER_QDIM // 2
PEER_TOPK = 16
PEER_BLOCK = 128
SPLITS = (RET_QK, RET_QK, RET_WIDTH, RET_WIDTH, GDN_WIDTH, GDN_WIDTH, GDN_WIDTH, GDN_WIDTH, GDN_HEADS, GDN_HEADS)
IN_COLS = sum(SPLITS)

kernel_name = 'hybrid_retention_gdn_peer'


def _rmsnorm(x, w):
    xf = x.astype(jnp.float32)
    y = xf * lax.rsqrt(jnp.mean(xf * xf, axis=-1, keepdims=True) + NORM_EPS)
    return (y * w.astype(jnp.float32)).astype(x.dtype)


def _heads(t, n_heads):
    b, t_len, _ = t.shape
    return t.reshape(b, t_len, n_heads, -1).transpose(0, 2, 1, 3)


def _rope(t):
    t_len, d = t.shape[-2], t.shape[-1]
    half = d // 2
    inv = ROPE_BASE ** (-jnp.arange(half, dtype=jnp.float32) / half)
    ang = jnp.arange(t_len, dtype=jnp.float32)[:, None] * inv[None, :]
    cos, sin = jnp.cos(ang), jnp.sin(ang)
    t1, t2 = t[..., :half], t[..., half:]
    return jnp.concatenate([t1 * cos - t2 * sin, t1 * sin + t2 * cos], axis=-1)


def _retention(q, k, v):
    b, h, t_len, dk = q.shape
    n = t_len // CHUNK
    lg = jnp.log1p(-jnp.exp2(-5.0 - jnp.arange(h, dtype=jnp.float32)))
    k = k * (dk ** -0.5)
    q = q.reshape(b, h, n, CHUNK, dk)
    k = k.reshape(b, h, n, CHUNK, dk)
    v = v.reshape(b, h, n, CHUNK, -1)
    idx = jnp.arange(CHUNK, dtype=jnp.float32)
    diff = idx[:, None] - idx[None, :]
    decay = jnp.where(diff >= 0, jnp.exp(lg[:, None, None] * jnp.maximum(diff, 0.0)), 0.0)
    scores = jnp.einsum('bhnid,bhnjd->bhnij', q, k) * decay[None, :, None]
    intra = jnp.einsum('bhnij,bhnje->bhnie', scores, v)
    k_dec = k * jnp.exp(lg[:, None] * (CHUNK - 1.0 - idx))[None, :, None, :, None]
    chunk_kv = jnp.einsum('bhncd,bhnce->nbhde', k_dec, v)
    chunk_decay = jnp.exp(lg * CHUNK)[None, :, None, None]

    def step(state, kv):
        return state * chunk_decay + kv, state

    _, s_prev = lax.scan(step, jnp.zeros_like(chunk_kv[0]), chunk_kv)
    q_dec = q * jnp.exp(lg[:, None] * (idx + 1.0))[None, :, None, :, None]
    inter = jnp.einsum('bhncd,nbhde->bhnce', q_dec, s_prev)
    return (intra + inter).reshape(b, h, t_len, -1)


def _gated_delta_rule(q, k, v, g, beta):
    b, h, t_len, dk = q.shape
    dv = v.shape[-1]
    n = t_len // CHUNK
    q = (q * (dk ** -0.5)).reshape(b, h, n, CHUNK, dk)
    k = k.reshape(b, h, n, CHUNK, dk)
    v = v.reshape(b, h, n, CHUNK, dv)
    g = g.reshape(b, h, n, CHUNK)
    beta = beta.reshape(b, h, n, CHUNK)
    gc = jnp.cumsum(g, axis=-1)
    causal = jnp.tril(jnp.ones((CHUNK, CHUNK), dtype=bool))
    strict = jnp.tril(jnp.ones((CHUNK, CHUNK), dtype=bool), -1)
    gdiff = gc[..., :, None] - gc[..., None, :]
    lmask = jnp.where(causal, jnp.exp(jnp.where(causal, gdiff, 0.0)), 0.0)
    kb = k * beta[..., None]
    kkt = jnp.einsum('bhnid,bhnjd->bhnij', kb, k) * lmask
    m = jnp.eye(CHUNK, dtype=kkt.dtype) + jnp.where(strict, kkt, 0.0)
    rhs = jnp.concatenate([v * beta[..., None], kb * jnp.exp(gc)[..., None]], axis=-1)
    sol = lax.linalg.triangular_solve(m, rhs, left_side=True, lower=True, unit_diagonal=True)
    u, w = sol[..., :dv], sol[..., dv:]
    attn = jnp.einsum('bhnid,bhnjd->bhnij', q, k) * lmask
    q_dec = q * jnp.exp(gc)[..., None]
    glast = gc[..., -1:]
    k_st = k * jnp.exp(glast - gc)[..., None]
    dec = jnp.exp(glast[..., 0])
    xs = tuple(jnp.moveaxis(t, 2, 0) for t in (q_dec, attn, u, w, k_st, dec))

    def step(state, inp):
        qd, at, uc, wc, ks, dc = inp
        v_new = uc - jnp.einsum('bhcd,bhde->bhce', wc, state)
        o = jnp.einsum('bhcd,bhde->bhce', qd, state) + jnp.einsum('bhij,bhje->bhie', at, v_new)
        state = state * dc[..., None, None] + jnp.einsum('bhcd,bhce->bhde', ks, v_new)
        return state, o

    s0 = jnp.zeros((b, h, dk, dv), dtype=q.dtype)
    _, o = lax.scan(step, s0, xs)
    return jnp.moveaxis(o, 0, 2).reshape(b, h, t_len, dv)


def _causal_short_conv(x, w):
    ch = x.shape[-1]
    return lax.conv_general_dilated(x, w[:, None, :], window_strides=(1,), padding=[(CONV_K - 1, 0)],
                                    dimension_numbers=('NWC', 'WIO', 'NWC'), feature_group_count=ch)


def _peer(h, wq, sub_keys, u_tab, v_tab):
    b, t_len, d = h.shape
    hb = h.reshape(-1, PEER_BLOCK, d)

    def block(xb):
        q = (xb @ wq).reshape(PEER_BLOCK, PEER_HEADS, 2, PEER_HALF)
        s = jnp.einsum('thpd,hpkd->thpk', q, sub_keys).astype(jnp.float32)
        s1, i1 = lax.top_k(s[:, :, 0], PEER_TOPK)
        s2, i2 = lax.top_k(s[:, :, 1], PEER_TOPK)
        cand = (s1[..., :, None] + s2[..., None, :]).reshape(PEER_BLOCK, PEER_HEADS, -1)
        cid = (i1[..., :, None] * PEER_KEYS + i2[..., None, :]).reshape(PEER_BLOCK, PEER_HEADS, -1)
        best, pos = lax.top_k(cand, PEER_TOPK)
        ids = jnp.take_along_axis(cid, pos, axis=-1).reshape(PEER_BLOCK, -1)
        gate = jax.nn.softmax(best, axis=-1).reshape(PEER_BLOCK, -1)
        act = jax.nn.gelu(jnp.einsum('tkd,td->tk', u_tab[ids], xb).astype(jnp.float32), approximate=False)
        return jnp.einsum('tk,tkd->td', (gate * act).astype(xb.dtype), v_tab[ids])

    return lax.map(block, hb).reshape(b, t_len, d)


def setup_inputs(seed: int = 0) -> dict:
    key = jax.random.key(seed)
    ks = jax.random.split(key, 16)
    f32 = jnp.float32

    def nrm(k, shape, scale):
        return jax.random.normal(k, shape, f32) * scale

    x = nrm(ks[0], (BATCH, SEQ, D_MODEL), 1.0)
    ln1_w = 1.0 + nrm(ks[1], (DEPTH, D_MODEL), 0.02)
    w_in = nrm(ks[2], (DEPTH, D_MODEL, IN_COLS), D_MODEL ** -0.5)
    ret_norm_w = 1.0 + nrm(ks[3], (DEPTH, RET_WIDTH), 0.02)
    conv_w = nrm(ks[4], (DEPTH, CONV_K, CONV_CH), CONV_K ** -0.5)
    A_log = jnp.log(jax.random.uniform(ks[5], (DEPTH, GDN_HEADS), f32, 1.0, 16.0))
    dt = jnp.exp(jax.random.uniform(ks[6], (DEPTH, GDN_HEADS), f32, np.log(1e-3), np.log(1e-1)))
    dt_bias = dt + jnp.log(-jnp.expm1(-dt))
    gdn_norm_w = 1.0 + nrm(ks[7], (DEPTH, GDN_HEAD_DIM), 0.02)
    w_out = nrm(ks[8], (DEPTH, D_MIX, D_MODEL), D_MIX ** -0.5)
    ln2_w = 1.0 + nrm(ks[9], (DEPTH, D_MODEL), 0.02)
    peer_wq = nrm(ks[10], (DEPTH, D_MODEL, PEER_HEADS * PEER_QDIM), D_MODEL ** -0.5)
    peer_keys = nrm(ks[11], (DEPTH, PEER_HEADS, 2, PEER_KEYS, PEER_HALF), PEER_HALF ** -0.5)
    peer_u = nrm(ks[12], (DEPTH, PEER_EXPERTS, D_MODEL), D_MODEL ** -0.5)
    peer_v = nrm(ks[13], (DEPTH, PEER_EXPERTS, D_MODEL), PEER_HEADS ** -0.5)
    lnf_w = 1.0 + nrm(ks[14], (D_MODEL,), 0.02)
    return {'x': x, 'ln1_w': ln1_w, 'w_in': w_in, 'ret_norm_w': ret_norm_w, 'conv_w': conv_w,
            'A_log': A_log, 'dt_bias': dt_bias, 'gdn_norm_w': gdn_norm_w, 'w_out': w_out,
            'ln2_w': ln2_w, 'peer_wq': peer_wq, 'peer_keys': peer_keys, 'peer_u': peer_u,
            'peer_v': peer_v, 'lnf_w': lnf_w}


def reference(x, ln1_w, w_in, ret_norm_w, conv_w, A_log, dt_bias, gdn_norm_w, w_out,
              ln2_w, peer_wq, peer_keys, peer_u, peer_v, lnf_w):
    f32 = jnp.float32
    offsets = [int(o) for o in np.cumsum(SPLITS)[:-1]]
    for l in range(DEPTH):
        h = _rmsnorm(x, ln1_w[l])
        proj = (h @ w_in[l]).astype(f32)
        rq, rk, rv, rg, gq, gk, gv, gz, ga, gb = jnp.split(proj, offsets, axis=-1)

        q_r = _rope(_heads(rq, RET_HEADS))
        k_r = _rope(_heads(rk, RET_HEADS))
        o_r = _retention(q_r, k_r, _heads(rv, RET_HEADS))
        mu = jnp.mean(o_r, axis=-1, keepdims=True)
        var = jnp.mean(jnp.square(o_r - mu), axis=-1, keepdims=True)
        o_r = ((o_r - mu) * lax.rsqrt(var + NORM_EPS)).transpose(0, 2, 1, 3).reshape(x.shape[0], x.shape[1], RET_WIDTH)
        y_r = o_r * ret_norm_w[l].astype(f32) * jax.nn.silu(rg)

        qkv = jax.nn.silu(_causal_short_conv(jnp.concatenate([gq, gk, gv], axis=-1), conv_w[l].astype(f32)))
        cq, ck, cv = jnp.split(qkv, 3, axis=-1)
        q_g = _heads(cq, GDN_HEADS)
        k_g = _heads(ck, GDN_HEADS)
        q_g = q_g * lax.rsqrt(jnp.sum(q_g * q_g, axis=-1, keepdims=True) + NORM_EPS)
        k_g = k_g * lax.rsqrt(jnp.sum(k_g * k_g, axis=-1, keepdims=True) + NORM_EPS)
        g_dec = -jnp.exp(A_log[l].astype(f32)) * jax.nn.softplus(ga + dt_bias[l].astype(f32))
        beta = jax.nn.sigmoid(gb)
        o_g = _gated_delta_rule(q_g, k_g, _heads(cv, GDN_HEADS),
                                g_dec.transpose(0, 2, 1), beta.transpose(0, 2, 1))
        o_g = _rmsnorm(o_g.transpose(0, 2, 1, 3), gdn_norm_w[l].astype(f32))
        y_g = o_g.reshape(x.shape[0], x.shape[1], GDN_WIDTH) * jax.nn.silu(gz)

        mixed = jnp.concatenate([y_r, y_g], axis=-1).astype(x.dtype)
        x = x + mixed @ w_out[l]

        h2 = _rmsnorm(x, ln2_w[l])
        x = x + _peer(h2, peer_wq[l], peer_keys[l], peer_u[l], peer_v[l])
    return _rmsnorm(x, lnf_w)
```

```python
import jax
import jax.numpy as jnp
import numpy as np
from jax import lax
from jax.experimental import pallas as pl
from jax.experimental.pallas import tpu as pltpu

D_MODEL = 1024
RET_HEADS = 4
RET_QK_DIM = 64
RET_V_DIM = 128
RET_QK = RET_HEADS * RET_QK_DIM
RET_WIDTH = RET_HEADS * RET_V_DIM
GDN_HEADS = 4
GDN_HEAD_DIM = 128
GDN_WIDTH = GDN_HEADS * GDN_HEAD_DIM
CONV_K = 4
CHUNK = 128
ROPE_BASE = 10000.0
NORM_EPS = 1e-6
PEER_HEADS = 8
PEER_KEYS = 128
PEER_QDIM = 256
PEER_HALF = PEER_QDIM // 2
PEER_TOPK = 16
PEER_BLOCK = 128
SPLITS = (RET_QK, RET_QK, RET_WIDTH, RET_WIDTH, GDN_WIDTH, GDN_WIDTH, GDN_WIDTH, GDN_WIDTH, GDN_HEADS, GDN_HEADS)


def _rmsnorm(x, w):
    xf = x.astype(jnp.float32)
    y = xf * lax.rsqrt(jnp.mean(xf * xf, axis=-1, keepdims=True) + NORM_EPS)
    return (y * w.astype(jnp.float32)).astype(x.dtype)


def _heads(t, n_heads):
    b, t_len, _ = t.shape
    return t.reshape(b, t_len, n_heads, -1).transpose(0, 2, 1, 3)


def _rope(t):
    t_len, d = t.shape[-2], t.shape[-1]
    half = d // 2
    inv = ROPE_BASE ** (-jnp.arange(half, dtype=jnp.float32) / half)
    ang = jnp.arange(t_len, dtype=jnp.float32)[:, None] * inv[None, :]
    cos, sin = jnp.cos(ang), jnp.sin(ang)
    t1, t2 = t[..., :half], t[..., half:]
    return jnp.concatenate([t1 * cos - t2 * sin, t1 * sin + t2 * cos], axis=-1)


def _retention(q, k, v):
    b, h, t_len, dk = q.shape
    n = t_len // CHUNK
    lg = jnp.log1p(-jnp.exp2(-5.0 - jnp.arange(h, dtype=jnp.float32)))
    k = k * (dk ** -0.5)
    q = q.reshape(b, h, n, CHUNK, dk)
    k = k.reshape(b, h, n, CHUNK, dk)
    v = v.reshape(b, h, n, CHUNK, -1)
    idx = jnp.arange(CHUNK, dtype=jnp.float32)
    diff = idx[:, None] - idx[None, :]
    decay = jnp.where(diff >= 0, jnp.exp(lg[:, None, None] * jnp.maximum(diff, 0.0)), 0.0)
    scores = jnp.einsum('bhnid,bhnjd->bhnij', q, k) * decay[None, :, None]
    intra = jnp.einsum('bhnij,bhnje->bhnie', scores, v)
    k_dec = k * jnp.exp(lg[:, None] * (CHUNK - 1.0 - idx))[None, :, None, :, None]
    chunk_kv = jnp.einsum('bhncd,bhnce->nbhde', k_dec, v)
    chunk_decay = jnp.exp(lg * CHUNK)[None, :, None, None]

    def step(state, kv):
        return state * chunk_decay + kv, state

    _, s_prev = lax.scan(step, jnp.zeros_like(chunk_kv[0]), chunk_kv)
    q_dec = q * jnp.exp(lg[:, None] * (idx + 1.0))[None, :, None, :, None]
    inter = jnp.einsum('bhncd,nbhde->bhnce', q_dec, s_prev)
    return (intra + inter).reshape(b, h, t_len, -1)


def _gated_delta_rule(q, k, v, g, beta):
    b, h, t_len, dk = q.shape
    dv = v.shape[-1]
    n = t_len // CHUNK
    q = (q * (dk ** -0.5)).reshape(b, h, n, CHUNK, dk)
    k = k.reshape(b, h, n, CHUNK, dk)
    v = v.reshape(b, h, n, CHUNK, dv)
    g = g.reshape(b, h, n, CHUNK)
    beta = beta.reshape(b, h, n, CHUNK)
    gc = jnp.cumsum(g, axis=-1)
    causal = jnp.tril(jnp.ones((CHUNK, CHUNK), dtype=bool))
    strict = jnp.tril(jnp.ones((CHUNK, CHUNK), dtype=bool), -1)
    gdiff = gc[..., :, None] - gc[..., None, :]
    lmask = jnp.where(causal, jnp.exp(jnp.where(causal, gdiff, 0.0)), 0.0)
    kb = k * beta[..., None]
    kkt = jnp.einsum('bhnid,bhnjd->bhnij', kb, k) * lmask
    m = jnp.eye(CHUNK, dtype=kkt.dtype) + jnp.where(strict, kkt, 0.0)
    rhs = jnp.concatenate([v * beta[..., None], kb * jnp.exp(gc)[..., None]], axis=-1)
    sol = lax.linalg.triangular_solve(m, rhs, left_side=True, lower=True, unit_diagonal=True)
    u, w = sol[..., :dv], sol[..., dv:]
    attn = jnp.einsum('bhnid,bhnjd->bhnij', q, k) * lmask
    q_dec = q * jnp.exp(gc)[..., None]
    glast = gc[..., -1:]
    k_st = k * jnp.exp(glast - gc)[..., None]
    dec = jnp.exp(glast[..., 0])
    xs = tuple(jnp.moveaxis(t, 2, 0) for t in (q_dec, attn, u, w, k_st, dec))

    def step(state, inp):
        qd, at, uc, wc, ks, dc = inp
        v_new = uc - jnp.einsum('bhcd,bhde->bhce', wc, state)
        o = jnp.einsum('bhcd,bhde->bhce', qd, state) + jnp.einsum('bhij,bhje->bhie', at, v_new)
        state = state * dc[..., None, None] + jnp.einsum('bhcd,bhce->bhde', ks, v_new)
        return state, o

    s0 = jnp.zeros((b, h, dk, dv), dtype=q.dtype)
    _, o = lax.scan(step, s0, xs)
    return jnp.moveaxis(o, 0, 2).reshape(b, h, t_len, dv)


def _causal_short_conv(x, w):
    ch = x.shape[-1]
    return lax.conv_general_dilated(x, w[:, None, :], window_strides=(1,), padding=[(CONV_K - 1, 0)],
                                    dimension_numbers=('NWC', 'WIO', 'NWC'), feature_group_count=ch)


def _peer_route(h, wq, sub_keys):
    t = h.shape[0]
    q = (h @ wq).reshape(t, PEER_HEADS, 2, PEER_HALF)
    s = jnp.einsum('thpd,hpkd->thpk', q, sub_keys).astype(jnp.float32)
    s1, i1 = lax.top_k(s[:, :, 0], PEER_TOPK)
    s2, i2 = lax.top_k(s[:, :, 1], PEER_TOPK)
    cand = (s1[..., :, None] + s2[..., None, :]).reshape(t, PEER_HEADS, -1)
    cid = (i1[..., :, None] * PEER_KEYS + i2[..., None, :]).reshape(t, PEER_HEADS, -1)
    best, pos = lax.top_k(cand, PEER_TOPK)
    ids = jnp.take_along_axis(cid, pos, axis=-1).reshape(t, -1)
    gate = jax.nn.softmax(best, axis=-1).reshape(t, -1)
    return ids.astype(jnp.int32), gate


PEER_NK = PEER_HEADS * PEER_TOPK
PG_TB = 32
PG_SB = 8
PG_NSUB = PG_TB // PG_SB
PG_ROWS = PG_SB * PEER_NK
PG_NCH = 2 * D_MODEL // 128


def _pg_issue(ids_ref, base, uv_ref, buf, sem, slot):
    def body(j, _):
        for k in range(PEER_NK):
            e = ids_ref[base + j * PEER_NK + k]
            pltpu.make_async_copy(uv_ref.at[e], buf.at[slot, :, pl.ds(j * PEER_NK + k, 1), :], sem.at[slot]).start()
        return 0
    lax.fori_loop(0, PG_SB, body, 0)


def _peer_gather_kernel(ids_ref, idsn_ref, gate_ref, h2_ref, x1_ref, lnf_ref, uv_ref, o_ref, buf, sem):
    i = pl.program_id(0)
    n = pl.num_programs(0)

    @pl.when(i == 0)
    def _():
        _pg_issue(ids_ref, 0, uv_ref, buf, sem, 0)

    sub = lax.broadcasted_iota(jnp.int32, (PG_SB, PEER_NK), 0)
    lane_tok = lax.broadcasted_iota(jnp.int32, (PG_SB, PG_ROWS), 1) // PEER_NK
    sub_w = lax.broadcasted_iota(jnp.int32, (PG_SB, PG_ROWS), 0)
    for s in range(PG_NSUB):
        slot = s % 2
        nslot = (s + 1) % 2
        if s + 1 < PG_NSUB:
            _pg_issue(ids_ref, (s + 1) * PG_ROWS, uv_ref, buf, sem, nslot)
        else:
            @pl.when(i + 1 < n)
            def _():
                _pg_issue(idsn_ref, 0, uv_ref, buf, sem, nslot)
        pltpu.make_async_copy(buf.at[slot], buf.at[slot], sem.at[slot]).wait()
        rows = pl.ds(s * PG_SB, PG_SB)
        h8 = h2_ref[rows, :]
        u = jnp.concatenate([buf[slot, c] for c in range(PG_NCH // 2)], axis=1)
        dt = lax.dot_general(h8, u, (((1,), (1,)), ((), ())), preferred_element_type=jnp.float32)
        d = jnp.zeros((PG_SB, PEER_NK), jnp.float32)
        for j in range(PG_SB):
            d = d + jnp.where(sub == j, dt[:, j * PEER_NK:(j + 1) * PEER_NK], 0.0)
        act = 0.5 * d * (1.0 + lax.erf(d * (2.0 ** -0.5)))
        w = gate_ref[rows, :] * act
        wbd = jnp.where(lane_tok == sub_w, jnp.concatenate([w] * PG_SB, axis=1), 0.0)
        v = jnp.concatenate([buf[slot, c] for c in range(PG_NCH // 2, PG_NCH)], axis=1)
        y = x1_ref[rows, :] + jnp.dot(wbd, v, preferred_element_type=jnp.float32)
        ms = jnp.mean(y * y, axis=-1, keepdims=True)
        o_ref[rows, :] = y * lax.rsqrt(ms + NORM_EPS) * lnf_ref[...]


def _peer_gather(ids, gate, h2, x1, lnf_w, uv4):
    t = ids.shape[0]
    nsteps = t // PG_TB
    ids_flat = ids.reshape(t * PEER_NK)
    blk = PG_TB * PEER_NK
    return pl.pallas_call(
        _peer_gather_kernel,
        grid=(nsteps,),
        in_specs=[
            pl.BlockSpec((blk,), lambda i: (i,), memory_space=pltpu.SMEM),
            pl.BlockSpec((blk,), lambda i: (jnp.minimum(i + 1, nsteps - 1),), memory_space=pltpu.SMEM),
            pl.BlockSpec((PG_TB, PEER_NK), lambda i: (i, 0)),
            pl.BlockSpec((PG_TB, D_MODEL), lambda i: (i, 0)),
            pl.BlockSpec((PG_TB, D_MODEL), lambda i: (i, 0)),
            pl.BlockSpec((1, D_MODEL), lambda i: (0, 0)),
            pl.BlockSpec(memory_space=pl.ANY),
        ],
        out_specs=pl.BlockSpec((PG_TB, D_MODEL), lambda i: (i, 0)),
        out_shape=jax.ShapeDtypeStruct((t, D_MODEL), jnp.float32),
        scratch_shapes=[pltpu.VMEM((2, PG_NCH, PG_ROWS, 128), jnp.float32), pltpu.SemaphoreType.DMA((2,))],
        compiler_params=pltpu.CompilerParams(dimension_semantics=("arbitrary",), vmem_limit_bytes=40 * 1024 * 1024),
        name="peer_gather",
    )(ids_flat, ids_flat, gate, h2, x1, lnf_w.reshape(1, D_MODEL), uv4)


def kernel(x, ln1_w, w_in, ret_norm_w, conv_w, A_log, dt_bias, gdn_norm_w, w_out, ln2_w, peer_wq, peer_keys, peer_u, peer_v, lnf_w):
    f32 = jnp.float32
    offsets = [int(o) for o in np.cumsum(SPLITS)[:-1]]
    l = 0
    h = _rmsnorm(x, ln1_w[l])
    proj = (h @ w_in[l]).astype(f32)
    rq, rk, rv, rg, gq, gk, gv, gz, ga, gb = jnp.split(proj, offsets, axis=-1)
    q_r = _rope(_heads(rq, RET_HEADS))
    k_r = _rope(_heads(rk, RET_HEADS))
    o_r = _retention(q_r, k_r, _heads(rv, RET_HEADS))
    mu = jnp.mean(o_r, axis=-1, keepdims=True)
    var = jnp.mean(jnp.square(o_r - mu), axis=-1, keepdims=True)
    o_r = ((o_r - mu) * lax.rsqrt(var + NORM_EPS)).transpose(0, 2, 1, 3).reshape(x.shape[0], x.shape[1], RET_WIDTH)
    y_r = o_r * ret_norm_w[l].astype(f32) * jax.nn.silu(rg)
    qkv = jax.nn.silu(_causal_short_conv(jnp.concatenate([gq, gk, gv], axis=-1), conv_w[l].astype(f32)))
    cq, ck, cv = jnp.split(qkv, 3, axis=-1)
    q_g = _heads(cq, GDN_HEADS)
    k_g = _heads(ck, GDN_HEADS)
    q_g = q_g * lax.rsqrt(jnp.sum(q_g * q_g, axis=-1, keepdims=True) + NORM_EPS)
    k_g = k_g * lax.rsqrt(jnp.sum(k_g * k_g, axis=-1, keepdims=True) + NORM_EPS)
    g_dec = -jnp.exp(A_log[l].astype(f32)) * jax.nn.softplus(ga + dt_bias[l].astype(f32))
    beta = jax.nn.sigmoid(gb)
    o_g = _gated_delta_rule(q_g, k_g, _heads(cv, GDN_HEADS), g_dec.transpose(0, 2, 1), beta.transpose(0, 2, 1))
    o_g = _rmsnorm(o_g.transpose(0, 2, 1, 3), gdn_norm_w[l].astype(f32))
    y_g = o_g.reshape(x.shape[0], x.shape[1], GDN_WIDTH) * jax.nn.silu(gz)
    mixed = jnp.concatenate([y_r, y_g], axis=-1).astype(x.dtype)
    x = x + mixed @ w_out[l]
    h2 = _rmsnorm(x, ln2_w[l])
    x1 = x.reshape(-1, D_MODEL)
    h2 = h2.reshape(-1, D_MODEL)
    ids, gate = _peer_route(h2, peer_wq[l], peer_keys[l])
    ne = peer_u.shape[1]
    uv4 = jnp.concatenate([peer_u[l].reshape(ne, PG_NCH // 2, 1, 128), peer_v[l].reshape(ne, PG_NCH // 2, 1, 128)], axis=1)
    out = _peer_gather(ids, gate, h2, x1, lnf_w, uv4)
    return out.reshape(x.shape)
```

```python
import math

import jax
import jax.numpy as jnp
import numpy as np
from jax import lax
from jax.experimental import pallas as pl
from jax.experimental.pallas import tpu as pltpu

D_MODEL = 1024
RET_HEADS = 4
RET_QK_DIM = 64
RET_HALF = RET_QK_DIM // 2
RET_V_DIM = 128
RET_WIDTH = RET_HEADS * RET_V_DIM
GDN_HEADS = 4
GDN_HEAD_DIM = 128
GDN_WIDTH = GDN_HEADS * GDN_HEAD_DIM
CONV_K = 4
CONV_CH = 3 * GDN_WIDTH
CHUNK = 128
ROPE_BASE = 10000.0
NORM_EPS = 1e-6
PEER_HEADS = 8
PEER_KEYS = 128
PEER_HALF = 128
PEER_TOPK = 16
PEER_NK = PEER_HEADS * PEER_TOPK
LANES = 128
SUBLANES = 8
VMEM_LIMIT = 48 * 1024 * 1024


def _silu(x):
    return x * jax.nn.sigmoid(x)


IN_TM = 256
IN_WIDTHS = (2 * RET_HEADS * RET_QK_DIM, RET_WIDTH, RET_WIDTH, CONV_CH, GDN_WIDTH, LANES)


def _in_proj_cols():
    splits = (256, 256, 512, 512, 512, 512, 512, 512, 4, 4)
    rq, rk, rv, rg, gq, _, _, gz, ga, _ = np.cumsum((0,) + splits)[:10]

    def rotary_halves(base):
        first = [base + h * RET_QK_DIM + d for h in range(RET_HEADS) for d in range(RET_HALF)]
        second = [base + h * RET_QK_DIM + RET_HALF + d for h in range(RET_HEADS) for d in range(RET_HALF)]
        return first + second

    cols = rotary_halves(rq) + rotary_halves(rk)
    cols += list(range(rv, rv + RET_WIDTH)) + list(range(rg, rg + RET_WIDTH))
    cols += list(range(gq, gq + CONV_CH)) + list(range(gz, gz + GDN_WIDTH))
    cols += list(range(ga, ga + 2 * GDN_HEADS)) + [-1] * (LANES - 2 * GDN_HEADS)
    return np.asarray(cols, np.int32)


def _in_proj_kernel(x_ref, ln_ref, w_ref, *out_refs):
    x = x_ref[...]
    h = x * lax.rsqrt(jnp.mean(x * x, axis=-1, keepdims=True) + NORM_EPS) * ln_ref[...]
    p = jnp.dot(h.astype(jnp.bfloat16), w_ref[...], preferred_element_type=jnp.float32)
    off = 0
    for ref, width in zip(out_refs, IN_WIDTHS):
        ref[...] = p[:, off:off + width]
        off += width


def _in_proj(x2d, ln_w, w_p):
    t = x2d.shape[0]
    return pl.pallas_call(
        _in_proj_kernel,
        grid=(t // IN_TM,),
        in_specs=[pl.BlockSpec((IN_TM, D_MODEL), lambda i: (i, 0)),
                  pl.BlockSpec((1, D_MODEL), lambda i: (0, 0)),
                  pl.BlockSpec(w_p.shape, lambda i: (0, 0))],
        out_specs=[pl.BlockSpec((IN_TM, w), lambda i: (i, 0)) for w in IN_WIDTHS],
        out_shape=[jax.ShapeDtypeStruct((t, w), jnp.float32) for w in IN_WIDTHS],
        compiler_params=pltpu.CompilerParams(dimension_semantics=("arbitrary",), vmem_limit_bytes=VMEM_LIMIT),
        name="in_proj",
    )(x2d, ln_w.reshape(1, D_MODEL), w_p)


def _retention_kernel(rqk_ref, rv_ref, rg_ref, cos_ref, sin_ref, nw_ref, o_ref, state):
    c = CHUNK

    @pl.when(pl.program_id(0) == 0)
    def _():
        state[...] = jnp.zeros_like(state)

    cos, sin = cos_ref[...], sin_ref[...]
    qa, qb = rqk_ref[:, 0:128], rqk_ref[:, 128:256]
    ka, kb = rqk_ref[:, 256:384], rqk_ref[:, 384:512]
    q = jnp.concatenate([qa * cos - qb * sin, qa * sin + qb * cos], axis=1)
    k = jnp.concatenate([ka * cos - kb * sin, ka * sin + kb * cos], axis=1) * (RET_QK_DIM ** -0.5)
    lane_head = (lax.broadcasted_iota(jnp.int32, (1, 2 * LANES), 1) % LANES) // RET_HALF
    ri = lax.broadcasted_iota(jnp.int32, (c, c), 0)
    ci = lax.broadcasted_iota(jnp.int32, (c, c), 1)
    diff = (ri - ci).astype(jnp.float32)
    pos = lax.broadcasted_iota(jnp.int32, (c, 1), 0).astype(jnp.float32)
    for h in range(RET_HEADS):
        lg = math.log1p(-(2.0 ** (-5.0 - h)))
        head_lanes = lane_head == h
        qh = jnp.where(head_lanes, q, 0.0)
        kh = jnp.where(head_lanes, k, 0.0)
        sl = slice(h * RET_V_DIM, (h + 1) * RET_V_DIM)
        v = rv_ref[:, sl]
        decay = jnp.where(diff >= 0, jnp.exp(lg * jnp.maximum(diff, 0.0)), 0.0)
        scores = lax.dot_general(qh, k, (((1,), (1,)), ((), ())), preferred_element_type=jnp.float32) * decay
        intra = jnp.dot(scores, v, preferred_element_type=jnp.float32)
        s_prev = state[h]
        inter = jnp.dot(qh * jnp.exp(lg * (pos + 1.0)), s_prev, preferred_element_type=jnp.float32)
        o = intra + inter
        k_dec = kh * jnp.exp(lg * (c - 1.0 - pos))
        state[h] = s_prev * math.exp(lg * c) + lax.dot_general(k_dec, v, (((0,), (0,)), ((), ())),
                                                              preferred_element_type=jnp.float32)
        mu = jnp.mean(o, axis=-1, keepdims=True)
        var = jnp.mean(jnp.square(o - mu), axis=-1, keepdims=True)
        o_ref[:, sl] = (o - mu) * lax.rsqrt(var + NORM_EPS) * nw_ref[:, sl] * _silu(rg_ref[:, sl])


def _retention(rqk, rv, rg, cos4, sin4, norm_w):
    t = rqk.shape[0]
    c = CHUNK
    tok = lambda w: pl.BlockSpec((c, w), lambda i: (i, 0))
    return pl.pallas_call(
        _retention_kernel,
        grid=(t // c,),
        in_specs=[tok(512), tok(RET_WIDTH), tok(RET_WIDTH), tok(LANES), tok(LANES),
                  pl.BlockSpec((1, RET_WIDTH), lambda i: (0, 0))],
        out_specs=tok(RET_WIDTH),
        out_shape=jax.ShapeDtypeStruct((t, RET_WIDTH), jnp.float32),
        scratch_shapes=[pltpu.VMEM((RET_HEADS, 2 * LANES, RET_V_DIM), jnp.float32)],
        compiler_params=pltpu.CompilerParams(dimension_semantics=("arbitrary",)),
        name="retention",
    )(rqk, rv, rg, cos4, sin4, norm_w.reshape(1, RET_WIDTH))


def _rope_tables(t):
    inv = ROPE_BASE ** (-jnp.arange(RET_HALF, dtype=jnp.float32) / RET_HALF)
    ang = jnp.arange(t, dtype=jnp.float32)[:, None] * inv[None, :]
    return jnp.tile(jnp.cos(ang), (1, RET_HEADS)), jnp.tile(jnp.sin(ang), (1, RET_HEADS))


def _gdn_kernel(cur_ref, halo_ref, gab_ref, gz_ref, cw_ref, alog_ref, dtb_ref, nw_ref, o_ref, state):
    c = CHUNK
    i = pl.program_id(0)

    @pl.when(i == 0)
    def _():
        state[...] = jnp.zeros_like(state)

    halo = jnp.where(i > 0, halo_ref[...], 0.0)
    xe = jnp.concatenate([halo, cur_ref[...]], axis=0)
    first = SUBLANES - (CONV_K - 1)
    y = cw_ref[0:1, :] * xe[first:first + c]
    for j in range(1, CONV_K):
        y = y + cw_ref[j:j + 1, :] * xe[first + j:first + j + c]
    qkv = _silu(y)

    gab = gab_ref[...]
    sp_in = gab + dtb_ref[...]
    softplus = jnp.maximum(sp_in, 0.0) + jnp.log1p(jnp.exp(-jnp.abs(sp_in)))
    g_all = -jnp.exp(alog_ref[...]) * softplus
    beta_all = jax.nn.sigmoid(gab)
    ri = lax.broadcasted_iota(jnp.int32, (c, c), 0)
    ci = lax.broadcasted_iota(jnp.int32, (c, c), 1)
    causal = ri >= ci
    strict = ri > ci
    eye = (ri == ci).astype(jnp.float32)
    gc_all = jnp.dot(causal.astype(jnp.float32), g_all, preferred_element_type=jnp.float32)
    gc_rows = jnp.dot(g_all.T, (ri <= ci).astype(jnp.float32), preferred_element_type=jnp.float32)
    for h in range(GDN_HEADS):
        d = GDN_HEAD_DIM
        sl = slice(h * d, (h + 1) * d)
        q = qkv[:, h * d:(h + 1) * d]
        k = qkv[:, GDN_WIDTH + h * d:GDN_WIDTH + (h + 1) * d]
        v = qkv[:, 2 * GDN_WIDTH + h * d:2 * GDN_WIDTH + (h + 1) * d]
        q = q * lax.rsqrt(jnp.sum(q * q, axis=-1, keepdims=True) + NORM_EPS) * (d ** -0.5)
        k = k * lax.rsqrt(jnp.sum(k * k, axis=-1, keepdims=True) + NORM_EPS)
        gc = gc_all[:, h:h + 1]
        gr = gc_rows[h:h + 1, :]
        beta = beta_all[:, GDN_HEADS + h:GDN_HEADS + h + 1]
        lmask = jnp.where(causal, jnp.exp(jnp.where(causal, gc - gr, 0.0)), 0.0)
        kb = k * beta
        a = jnp.where(strict, lax.dot_general(kb, k, (((1,), (1,)), ((), ())), preferred_element_type=jnp.float32) * lmask, 0.0)
        x = eye - jnp.where(ri // 2 == ci // 2, a, 0.0)
        b = 2
        while b < c:
            off = (ri // (2 * b) == ci // (2 * b)) & (ri % (2 * b) >= b) & (ci % (2 * b) < b)
            xa = jnp.dot(x, jnp.where(off, a, 0.0), preferred_element_type=jnp.float32)
            x = x - jnp.dot(xa, x, preferred_element_type=jnp.float32)
            b *= 2
        egc = jnp.exp(gc)
        sol = jnp.dot(x, jnp.concatenate([v * beta, kb * egc], axis=1), preferred_element_type=jnp.float32)
        u, w = sol[:, :d], sol[:, d:]
        attn = lax.dot_general(q, k, (((1,), (1,)), ((), ())), preferred_element_type=jnp.float32) * lmask
        glast = gc_all[c - 1:c, h:h + 1]
        s = state[h]
        v_new = u - jnp.dot(w, s, preferred_element_type=jnp.float32)
        o = jnp.dot(q * egc, s, preferred_element_type=jnp.float32) + jnp.dot(attn, v_new, preferred_element_type=jnp.float32)
        k_st = k * jnp.exp(glast - gc)
        state[h] = s * jnp.exp(glast) + lax.dot_general(k_st, v_new, (((0,), (0,)), ((), ())),
                                                        preferred_element_type=jnp.float32)
        on = o * lax.rsqrt(jnp.mean(o * o, axis=-1, keepdims=True) + NORM_EPS) * nw_ref[...]
        o_ref[:, sl] = on * _silu(gz_ref[:, sl])


def _gdn(gqkv, gab, gz, conv_w, a_log, dt_bias, norm_w):
    t = gqkv.shape[0]
    c = CHUNK
    lane_pad = lambda v: jnp.zeros((1, LANES), jnp.float32).at[0, :GDN_HEADS].set(v)
    tok = lambda w: pl.BlockSpec((c, w), lambda i: (i, 0))
    const = lambda shape: pl.BlockSpec(shape, lambda i: (0, 0))
    return pl.pallas_call(
        _gdn_kernel,
        grid=(t // c,),
        in_specs=[tok(CONV_CH),
                  pl.BlockSpec((SUBLANES, CONV_CH), lambda i: (jnp.maximum(i * (c // SUBLANES) - 1, 0), 0)),
                  tok(LANES), tok(GDN_WIDTH), const((CONV_K, CONV_CH)), const((1, LANES)), const((1, LANES)),
                  const((1, GDN_HEAD_DIM))],
        out_specs=tok(GDN_WIDTH),
        out_shape=jax.ShapeDtypeStruct((t, GDN_WIDTH), jnp.float32),
        scratch_shapes=[pltpu.VMEM((GDN_HEADS, GDN_HEAD_DIM, GDN_HEAD_DIM), jnp.float32)],
        compiler_params=pltpu.CompilerParams(dimension_semantics=("arbitrary",)),
        name="gdn",
    )(gqkv, gqkv, gab, gz, conv_w, lane_pad(a_log), lane_pad(dt_bias), norm_w.reshape(1, GDN_HEAD_DIM))


MID_TM = 256


def _mid_kernel(x_ref, yr_ref, yg_ref, wo_ref, ln_ref, wq_ref, keys_ref, x1_ref, h2_ref, st_ref):
    mixed = jnp.concatenate([yr_ref[...], yg_ref[...]], axis=1).astype(jnp.bfloat16)
    x1 = x_ref[...] + jnp.dot(mixed, wo_ref[...], preferred_element_type=jnp.float32)
    x1_ref[...] = x1
    h2 = x1 * lax.rsqrt(jnp.mean(x1 * x1, axis=-1, keepdims=True) + NORM_EPS) * ln_ref[...]
    h2_ref[...] = h2
    q = jnp.dot(h2.astype(jnp.bfloat16), wq_ref[...], preferred_element_type=jnp.float32).astype(jnp.bfloat16)
    for hp in range(2 * PEER_HEADS):
        st_ref[hp] = lax.dot_general(keys_ref[hp], q[:, hp * PEER_HALF:(hp + 1) * PEER_HALF], (((1,), (1,)), ((), ())),
                                     preferred_element_type=jnp.float32)


def _mid_proj(x2d, y_r, y_g, w_out_b, ln2_w, wq_b, keys_b):
    t = x2d.shape[0]
    tok = lambda w: pl.BlockSpec((MID_TM, w), lambda i: (i, 0))
    nhp = 2 * PEER_HEADS
    return pl.pallas_call(
        _mid_kernel,
        grid=(t // MID_TM,),
        in_specs=[tok(D_MODEL), tok(RET_WIDTH), tok(GDN_WIDTH),
                  pl.BlockSpec((D_MODEL, D_MODEL), lambda i: (0, 0)),
                  pl.BlockSpec((1, D_MODEL), lambda i: (0, 0)),
                  pl.BlockSpec(wq_b.shape, lambda i: (0, 0)),
                  pl.BlockSpec((nhp, PEER_KEYS, PEER_HALF), lambda i: (0, 0, 0))],
        out_specs=[tok(D_MODEL), tok(D_MODEL), pl.BlockSpec((nhp, PEER_KEYS, MID_TM), lambda i: (0, 0, i))],
        out_shape=[jax.ShapeDtypeStruct((t, D_MODEL), jnp.float32),
                   jax.ShapeDtypeStruct((t, D_MODEL), jnp.float32),
                   jax.ShapeDtypeStruct((nhp, PEER_KEYS, t), jnp.float32)],
        compiler_params=pltpu.CompilerParams(dimension_semantics=("arbitrary",), vmem_limit_bytes=VMEM_LIMIT),
        name="mid_proj",
    )(x2d, y_r, y_g, w_out_b, ln2_w.reshape(1, D_MODEL), wq_b, keys_b)


TOPK_TL = LANES
_PAIRS = [(a, b) for a in range(PEER_TOPK) for b in range(PEER_TOPK) if (a + 1) * (b + 1) <= PEER_TOPK]
_NPAIR_PAD = -(-len(_PAIRS) // SUBLANES) * SUBLANES
_NCAND = PEER_TOPK * PEER_TOPK


def _top16(s, n):
    iota = lax.broadcasted_iota(jnp.int32, s.shape, 0)
    vals, idxs = [], []
    for _ in range(PEER_TOPK):
        m = jnp.max(s, axis=0, keepdims=True)
        idx = jnp.min(jnp.where(s == m, iota, n), axis=0, keepdims=True)
        vals.append(m)
        idxs.append(idx)
        s = jnp.where(iota == idx, -jnp.inf, s)
    return jnp.concatenate(vals, axis=0), jnp.concatenate(idxs, axis=0)


def _topk_kernel(st_ref, ids_ref, gate_ref):
    tl = st_ref.shape[2]
    npad = _NPAIR_PAD - len(_PAIRS)
    riota = lax.broadcasted_iota(jnp.int32, (_NPAIR_PAD, tl), 0)
    pos = jnp.full((_NPAIR_PAD, tl), _NCAND, jnp.int32)
    for r, (a, b) in enumerate(_PAIRS):
        pos = jnp.where(riota == r, a * PEER_TOPK + b, pos)
    ids_rows, gate_rows = [], []
    for h in range(PEER_HEADS):
        s1, i1 = _top16(st_ref[2 * h], PEER_KEYS)
        s2, i2 = _top16(st_ref[2 * h + 1], PEER_KEYS)
        cand = jnp.concatenate([s1[a:a + 1, :] + s2[b:b + 1, :] for a, b in _PAIRS]
                               + [jnp.full((npad, tl), -jnp.inf, jnp.float32)], axis=0)
        cid = jnp.concatenate([i1[a:a + 1, :] * PEER_KEYS + i2[b:b + 1, :] for a, b in _PAIRS]
                              + [jnp.zeros((npad, tl), jnp.int32)], axis=0)
        best, sel = [], []
        for _ in range(PEER_TOPK):
            m = jnp.max(cand, axis=0, keepdims=True)
            p = jnp.min(jnp.where(cand == m, pos, _NCAND + 1), axis=0, keepdims=True)
            hit = pos == p
            best.append(m)
            sel.append(jnp.max(jnp.where(hit, cid, -1), axis=0, keepdims=True))
            cand = jnp.where(hit, -jnp.inf, cand)
        best = jnp.concatenate(best, axis=0)
        e = jnp.exp(best - best[0:1, :])
        gate_rows.append(e / jnp.sum(e, axis=0, keepdims=True))
        ids_rows.append(jnp.concatenate(sel, axis=0))
    ids_ref[...] = jnp.concatenate(ids_rows, axis=0).T
    gate_ref[...] = jnp.concatenate(gate_rows, axis=0).T


def _peer_topk(st):
    t = st.shape[2]
    return pl.pallas_call(
        _topk_kernel,
        grid=(t // TOPK_TL,),
        in_specs=[pl.BlockSpec((2 * PEER_HEADS, PEER_KEYS, TOPK_TL), lambda i: (0, 0, i))],
        out_specs=[pl.BlockSpec((TOPK_TL, PEER_NK), lambda i: (i, 0)),
                   pl.BlockSpec((TOPK_TL, PEER_NK), lambda i: (i, 0))],
        out_shape=[jax.ShapeDtypeStruct((t, PEER_NK), jnp.int32),
                   jax.ShapeDtypeStruct((t, PEER_NK), jnp.float32)],
        compiler_params=pltpu.CompilerParams(dimension_semantics=("arbitrary",)),
        name="peer_topk",
    )(st)


PG_TB = 32
PG_SB = SUBLANES
PG_NSUB = PG_TB // PG_SB
PG_ROWS = PG_SB * PEER_NK
PG_NCH = 2 * D_MODEL // LANES


def _pg_issue(ids_ref, base, uv_ref, buf, sem, slot):
    def body(j, _):
        for k in range(PEER_NK):
            e = ids_ref[base + j * PEER_NK + k]
            pltpu.make_async_copy(uv_ref.at[e], buf.at[slot, :, pl.ds(j * PEER_NK + k, 1), :], sem.at[slot]).start()
        return 0
    lax.fori_loop(0, PG_SB, body, 0)


def _peer_gather_kernel(ids_ref, idsn_ref, gate_ref, h2_ref, x1_ref, lnf_ref, uv_ref, o_ref, buf, sem):
    i = pl.program_id(0)
    n = pl.num_programs(0)

    @pl.when(i == 0)
    def _():
        _pg_issue(ids_ref, 0, uv_ref, buf, sem, 0)

    sub = lax.broadcasted_iota(jnp.int32, (PG_SB, PEER_NK), 0)
    lane_tok = lax.broadcasted_iota(jnp.int32, (PG_SB, PG_ROWS), 1) // PEER_NK
    sub_w = lax.broadcasted_iota(jnp.int32, (PG_SB, PG_ROWS), 0)
    for s in range(PG_NSUB):
        slot = s % 2
        nslot = (s + 1) % 2
        if s + 1 < PG_NSUB:
            _pg_issue(ids_ref, (s + 1) * PG_ROWS, uv_ref, buf, sem, nslot)
        else:
            @pl.when(i + 1 < n)
            def _():
                _pg_issue(idsn_ref, 0, uv_ref, buf, sem, nslot)
        pltpu.make_async_copy(buf.at[slot], buf.at[slot], sem.at[slot]).wait()
        rows = pl.ds(s * PG_SB, PG_SB)
        h8 = h2_ref[rows, :]
        u = jnp.concatenate([buf[slot, c] for c in range(PG_NCH // 2)], axis=1)
        dt = lax.dot_general(h8, u, (((1,), (1,)), ((), ())), preferred_element_type=jnp.float32)
        d = jnp.zeros((PG_SB, PEER_NK), jnp.float32)
        for j in range(PG_SB):
            d = d + jnp.where(sub == j, dt[:, j * PEER_NK:(j + 1) * PEER_NK], 0.0)
        act = 0.5 * d * (1.0 + lax.erf(d * (2.0 ** -0.5)))
        w = gate_ref[rows, :] * act
        wbd = jnp.where(lane_tok == sub_w, jnp.concatenate([w] * PG_SB, axis=1), 0.0)
        v = jnp.concatenate([buf[slot, c] for c in range(PG_NCH // 2, PG_NCH)], axis=1)
        y = x1_ref[rows, :] + jnp.dot(wbd, v, preferred_element_type=jnp.float32)
        ms = jnp.mean(y * y, axis=-1, keepdims=True)
        o_ref[rows, :] = y * lax.rsqrt(ms + NORM_EPS) * lnf_ref[...]


def _peer_gather(ids, gate, h2, x1, lnf_w, uv4):
    t = ids.shape[0]
    nsteps = t // PG_TB
    ids_flat = ids.reshape(t * PEER_NK)
    blk = PG_TB * PEER_NK
    return pl.pallas_call(
        _peer_gather_kernel,
        grid=(nsteps,),
        in_specs=[
            pl.BlockSpec((blk,), lambda i: (i,), memory_space=pltpu.SMEM),
            pl.BlockSpec((blk,), lambda i: (jnp.minimum(i + 1, nsteps - 1),), memory_space=pltpu.SMEM),
            pl.BlockSpec((PG_TB, PEER_NK), lambda i: (i, 0)),
            pl.BlockSpec((PG_TB, D_MODEL), lambda i: (i, 0)),
            pl.BlockSpec((PG_TB, D_MODEL), lambda i: (i, 0)),
            pl.BlockSpec((1, D_MODEL), lambda i: (0, 0)),
            pl.BlockSpec(memory_space=pl.ANY),
        ],
        out_specs=pl.BlockSpec((PG_TB, D_MODEL), lambda i: (i, 0)),
        out_shape=jax.ShapeDtypeStruct((t, D_MODEL), jnp.float32),
        scratch_shapes=[pltpu.VMEM((2, PG_NCH, PG_ROWS, LANES), jnp.float32), pltpu.SemaphoreType.DMA((2,))],
        compiler_params=pltpu.CompilerParams(dimension_semantics=("arbitrary",), vmem_limit_bytes=VMEM_LIMIT),
        name="peer_gather",
    )(ids_flat, ids_flat, gate, h2, x1, lnf_w.reshape(1, D_MODEL), uv4)


def kernel(x, ln1_w, w_in, ret_norm_w, conv_w, A_log, dt_bias, gdn_norm_w, w_out, ln2_w, peer_wq, peer_keys, peer_u, peer_v, lnf_w):
    b, t, d = x.shape
    assert b == 1 and d == D_MODEL and t % MID_TM == 0 and ln1_w.shape[0] == 1
    l = 0
    x2d = x.reshape(t, d)

    cols = _in_proj_cols()
    w_p = jnp.where((cols >= 0)[None, :], jnp.take(w_in[l], jnp.maximum(cols, 0), axis=1), 0.0).astype(jnp.bfloat16)
    ne = peer_u.shape[1]
    uv4 = jnp.concatenate([peer_u[l].reshape(ne, PG_NCH // 2, 1, LANES), peer_v[l].reshape(ne, PG_NCH // 2, 1, LANES)], axis=1)
    keys_b = peer_keys[l].reshape(2 * PEER_HEADS, PEER_KEYS, PEER_HALF).astype(jnp.bfloat16)
    cos4, sin4 = _rope_tables(t)

    rqk, rv, rg, gqkv, gz, gab = _in_proj(x2d, ln1_w[l], w_p)
    y_r = _retention(rqk, rv, rg, cos4, sin4, ret_norm_w[l])
    y_g = _gdn(gqkv, gab, gz, conv_w[l], A_log[l], dt_bias[l], gdn_norm_w[l])
    x1, h2, st = _mid_proj(x2d, y_r, y_g, w_out[l].astype(jnp.bfloat16), ln2_w[l], peer_wq[l].astype(jnp.bfloat16), keys_b)
    ids, gate = _peer_topk(st)
    out = _peer_gather(ids, gate, h2, x1, lnf_w, uv4)
    return out.reshape(b, t, d)
```

```python
import math

import jax
import jax.numpy as jnp
import numpy as np
from jax import lax
import functools

from jax.experimental import pallas as pl
from jax.experimental.pallas import tpu as pltpu
from jax.experimental.pallas import tpu_sc as plsc

D_MODEL = 1024
RET_HEADS = 4
RET_QK_DIM = 64
RET_HALF = RET_QK_DIM // 2
RET_V_DIM = 128
RET_WIDTH = RET_HEADS * RET_V_DIM
GDN_HEADS = 4
GDN_HEAD_DIM = 128
GDN_WIDTH = GDN_HEADS * GDN_HEAD_DIM
CONV_K = 4
CONV_CH = 3 * GDN_WIDTH
CHUNK = 128
ROPE_BASE = 10000.0
NORM_EPS = 1e-6
PEER_HEADS = 8
PEER_KEYS = 128
PEER_HALF = 128
PEER_TOPK = 16
PEER_NK = PEER_HEADS * PEER_TOPK
LANES = 128
SUBLANES = 8
VMEM_LIMIT = 48 * 1024 * 1024


def _silu(x):
    return x * jax.nn.sigmoid(x)


IN_TM = 256
IN_WIDTHS = (2 * RET_HEADS * RET_QK_DIM, RET_WIDTH, RET_WIDTH, CONV_CH, GDN_WIDTH, LANES)


def _in_proj_cols():
    splits = (256, 256, 512, 512, 512, 512, 512, 512, 4, 4)
    rq, rk, rv, rg, gq, _, _, gz, ga, _ = np.cumsum((0,) + splits)[:10]

    def rotary_halves(base):
        first = [base + h * RET_QK_DIM + d for h in range(RET_HEADS) for d in range(RET_HALF)]
        second = [base + h * RET_QK_DIM + RET_HALF + d for h in range(RET_HEADS) for d in range(RET_HALF)]
        return first + second

    cols = rotary_halves(rq) + rotary_halves(rk)
    cols += list(range(rv, rv + RET_WIDTH)) + list(range(rg, rg + RET_WIDTH))
    cols += list(range(gq, gq + CONV_CH)) + list(range(gz, gz + GDN_WIDTH))
    cols += list(range(ga, ga + 2 * GDN_HEADS)) + [-1] * (LANES - 2 * GDN_HEADS)
    return np.asarray(cols, np.int32)


def _in_proj_kernel(x_ref, ln_ref, w_ref, *out_refs):
    x = x_ref[...]
    h = x * lax.rsqrt(jnp.mean(x * x, axis=-1, keepdims=True) + NORM_EPS) * ln_ref[...]
    p = jnp.dot(h.astype(jnp.bfloat16), w_ref[...], preferred_element_type=jnp.float32)
    off = 0
    for ref, width in zip(out_refs, IN_WIDTHS):
        ref[...] = p[:, off:off + width]
        off += width


def _in_proj(x2d, ln_w, w_p):
    t = x2d.shape[0]
    return pl.pallas_call(
        _in_proj_kernel,
        grid=(t // IN_TM,),
        in_specs=[pl.BlockSpec((IN_TM, D_MODEL), lambda i: (i, 0)),
                  pl.BlockSpec((1, D_MODEL), lambda i: (0, 0)),
                  pl.BlockSpec(w_p.shape, lambda i: (0, 0))],
        out_specs=[pl.BlockSpec((IN_TM, w), lambda i: (i, 0)) for w in IN_WIDTHS],
        out_shape=[jax.ShapeDtypeStruct((t, w), jnp.float32) for w in IN_WIDTHS],
        compiler_params=pltpu.CompilerParams(dimension_semantics=("arbitrary",), vmem_limit_bytes=VMEM_LIMIT),
        name="in_proj",
    )(x2d, ln_w.reshape(1, D_MODEL), w_p)


def _retention_kernel(rqk_ref, rv_ref, rg_ref, cos_ref, sin_ref, nw_ref, o_ref, state):
    c = CHUNK

    @pl.when(pl.program_id(0) == 0)
    def _():
        state[...] = jnp.zeros_like(state)

    cos, sin = cos_ref[...], sin_ref[...]
    qa, qb = rqk_ref[:, 0:128], rqk_ref[:, 128:256]
    ka, kb = rqk_ref[:, 256:384], rqk_ref[:, 384:512]
    q = jnp.concatenate([qa * cos - qb * sin, qa * sin + qb * cos], axis=1)
    k = jnp.concatenate([ka * cos - kb * sin, ka * sin + kb * cos], axis=1) * (RET_QK_DIM ** -0.5)
    lane_head = (lax.broadcasted_iota(jnp.int32, (1, 2 * LANES), 1) % LANES) // RET_HALF
    ri = lax.broadcasted_iota(jnp.int32, (c, c), 0)
    ci = lax.broadcasted_iota(jnp.int32, (c, c), 1)
    diff = (ri - ci).astype(jnp.float32)
    pos = lax.broadcasted_iota(jnp.int32, (c, 1), 0).astype(jnp.float32)
    for h in range(RET_HEADS):
        lg = math.log1p(-(2.0 ** (-5.0 - h)))
        head_lanes = lane_head == h
        qh = jnp.where(head_lanes, q, 0.0)
        kh = jnp.where(head_lanes, k, 0.0)
        sl = slice(h * RET_V_DIM, (h + 1) * RET_V_DIM)
        v = rv_ref[:, sl]
        decay = jnp.where(diff >= 0, jnp.exp(lg * jnp.maximum(diff, 0.0)), 0.0)
        scores = lax.dot_general(qh, k, (((1,), (1,)), ((), ())), preferred_element_type=jnp.float32) * decay
        intra = jnp.dot(scores, v, preferred_element_type=jnp.float32)
        s_prev = state[h]
        inter = jnp.dot(qh * jnp.exp(lg * (pos + 1.0)), s_prev, preferred_element_type=jnp.float32)
        o = intra + inter
        k_dec = kh * jnp.exp(lg * (c - 1.0 - pos))
        state[h] = s_prev * math.exp(lg * c) + lax.dot_general(k_dec, v, (((0,), (0,)), ((), ())),
                                                              preferred_element_type=jnp.float32)
        mu = jnp.mean(o, axis=-1, keepdims=True)
        var = jnp.mean(jnp.square(o - mu), axis=-1, keepdims=True)
        o_ref[:, sl] = (o - mu) * lax.rsqrt(var + NORM_EPS) * nw_ref[:, sl] * _silu(rg_ref[:, sl])


def _retention(rqk, rv, rg, cos4, sin4, norm_w):
    t = rqk.shape[0]
    c = CHUNK
    tok = lambda w: pl.BlockSpec((c, w), lambda i: (i, 0))
    return pl.pallas_call(
        _retention_kernel,
        grid=(t // c,),
        in_specs=[tok(512), tok(RET_WIDTH), tok(RET_WIDTH), tok(LANES), tok(LANES),
                  pl.BlockSpec((1, RET_WIDTH), lambda i: (0, 0))],
        out_specs=tok(RET_WIDTH),
        out_shape=jax.ShapeDtypeStruct((t, RET_WIDTH), jnp.float32),
        scratch_shapes=[pltpu.VMEM((RET_HEADS, 2 * LANES, RET_V_DIM), jnp.float32)],
        compiler_params=pltpu.CompilerParams(dimension_semantics=("arbitrary",)),
        name="retention",
    )(rqk, rv, rg, cos4, sin4, norm_w.reshape(1, RET_WIDTH))


def _rope_tables(t):
    inv = ROPE_BASE ** (-jnp.arange(RET_HALF, dtype=jnp.float32) / RET_HALF)
    ang = jnp.arange(t, dtype=jnp.float32)[:, None] * inv[None, :]
    return jnp.tile(jnp.cos(ang), (1, RET_HEADS)), jnp.tile(jnp.sin(ang), (1, RET_HEADS))


def _gdn_kernel(cur_ref, halo_ref, gab_ref, gz_ref, cw_ref, alog_ref, dtb_ref, nw_ref, o_ref, state):
    c = CHUNK
    i = pl.program_id(0)

    @pl.when(i == 0)
    def _():
        state[...] = jnp.zeros_like(state)

    halo = jnp.where(i > 0, halo_ref[...], 0.0)
    xe = jnp.concatenate([halo, cur_ref[...]], axis=0)
    first = SUBLANES - (CONV_K - 1)
    y = cw_ref[0:1, :] * xe[first:first + c]
    for j in range(1, CONV_K):
        y = y + cw_ref[j:j + 1, :] * xe[first + j:first + j + c]
    qkv = _silu(y)

    gab = gab_ref[...]
    sp_in = gab + dtb_ref[...]
    softplus = jnp.maximum(sp_in, 0.0) + jnp.log1p(jnp.exp(-jnp.abs(sp_in)))
    g_all = -jnp.exp(alog_ref[...]) * softplus
    beta_all = jax.nn.sigmoid(gab)
    ri = lax.broadcasted_iota(jnp.int32, (c, c), 0)
    ci = lax.broadcasted_iota(jnp.int32, (c, c), 1)
    causal = ri >= ci
    strict = ri > ci
    eye = (ri == ci).astype(jnp.float32)
    gc_all = jnp.dot(causal.astype(jnp.float32), g_all, preferred_element_type=jnp.float32)
    gc_rows = jnp.dot(g_all.T, (ri <= ci).astype(jnp.float32), preferred_element_type=jnp.float32)
    for h in range(GDN_HEADS):
        d = GDN_HEAD_DIM
        sl = slice(h * d, (h + 1) * d)
        q = qkv[:, h * d:(h + 1) * d]
        k = qkv[:, GDN_WIDTH + h * d:GDN_WIDTH + (h + 1) * d]
        v = qkv[:, 2 * GDN_WIDTH + h * d:2 * GDN_WIDTH + (h + 1) * d]
        q = q * lax.rsqrt(jnp.sum(q * q, axis=-1, keepdims=True) + NORM_EPS) * (d ** -0.5)
        k = k * lax.rsqrt(jnp.sum(k * k, axis=-1, keepdims=True) + NORM_EPS)
        gc = gc_all[:, h:h + 1]
        gr = gc_rows[h:h + 1, :]
        beta = beta_all[:, GDN_HEADS + h:GDN_HEADS + h + 1]
        lmask = jnp.where(causal, jnp.exp(jnp.where(causal, gc - gr, 0.0)), 0.0)
        kb = k * beta
        a = jnp.where(strict, lax.dot_general(kb, k, (((1,), (1,)), ((), ())), preferred_element_type=jnp.float32) * lmask, 0.0)
        x = eye - jnp.where(ri // 2 == ci // 2, a, 0.0)
        b = 2
        while b < c:
            off = (ri // (2 * b) == ci // (2 * b)) & (ri % (2 * b) >= b) & (ci % (2 * b) < b)
            xa = jnp.dot(x, jnp.where(off, a, 0.0), preferred_element_type=jnp.float32)
            x = x - jnp.dot(xa, x, preferred_element_type=jnp.float32)
            b *= 2
        egc = jnp.exp(gc)
        sol = jnp.dot(x, jnp.concatenate([v * beta, kb * egc], axis=1), preferred_element_type=jnp.float32)
        u, w = sol[:, :d], sol[:, d:]
        attn = lax.dot_general(q, k, (((1,), (1,)), ((), ())), preferred_element_type=jnp.float32) * lmask
        glast = gc_all[c - 1:c, h:h + 1]
        s = state[h]
        v_new = u - jnp.dot(w, s, preferred_element_type=jnp.float32)
        o = jnp.dot(q * egc, s, preferred_element_type=jnp.float32) + jnp.dot(attn, v_new, preferred_element_type=jnp.float32)
        k_st = k * jnp.exp(glast - gc)
        state[h] = s * jnp.exp(glast) + lax.dot_general(k_st, v_new, (((0,), (0,)), ((), ())),
                                                        preferred_element_type=jnp.float32)
        on = o * lax.rsqrt(jnp.mean(o * o, axis=-1, keepdims=True) + NORM_EPS) * nw_ref[...]
        o_ref[:, sl] = on * _silu(gz_ref[:, sl])


def _gdn(gqkv, gab, gz, conv_w, a_log, dt_bias, norm_w):
    t = gqkv.shape[0]
    c = CHUNK
    lane_pad = lambda v: jnp.zeros((1, LANES), jnp.float32).at[0, :GDN_HEADS].set(v)
    tok = lambda w: pl.BlockSpec((c, w), lambda i: (i, 0))
    const = lambda shape: pl.BlockSpec(shape, lambda i: (0, 0))
    return pl.pallas_call(
        _gdn_kernel,
        grid=(t // c,),
        in_specs=[tok(CONV_CH),
                  pl.BlockSpec((SUBLANES, CONV_CH), lambda i: (jnp.maximum(i * (c // SUBLANES) - 1, 0), 0)),
                  tok(LANES), tok(GDN_WIDTH), const((CONV_K, CONV_CH)), const((1, LANES)), const((1, LANES)),
                  const((1, GDN_HEAD_DIM))],
        out_specs=tok(GDN_WIDTH),
        out_shape=jax.ShapeDtypeStruct((t, GDN_WIDTH), jnp.float32),
        scratch_shapes=[pltpu.VMEM((GDN_HEADS, GDN_HEAD_DIM, GDN_HEAD_DIM), jnp.float32)],
        compiler_params=pltpu.CompilerParams(dimension_semantics=("arbitrary",)),
        name="gdn",
    )(gqkv, gqkv, gab, gz, conv_w, lane_pad(a_log), lane_pad(dt_bias), norm_w.reshape(1, GDN_HEAD_DIM))


MID_TM = 256


def _mid_kernel(x_ref, yr_ref, yg_ref, wo_ref, ln_ref, wq_ref, keys_ref, x1_ref, h2_ref, st_ref):
    mixed = jnp.concatenate([yr_ref[...], yg_ref[...]], axis=1).astype(jnp.bfloat16)
    x1 = x_ref[...] + jnp.dot(mixed, wo_ref[...], preferred_element_type=jnp.float32)
    x1_ref[...] = x1
    h2 = x1 * lax.rsqrt(jnp.mean(x1 * x1, axis=-1, keepdims=True) + NORM_EPS) * ln_ref[...]
    h2_ref[...] = h2
    q = jnp.dot(h2.astype(jnp.bfloat16), wq_ref[...], preferred_element_type=jnp.float32).astype(jnp.bfloat16)
    for hp in range(2 * PEER_HEADS):
        st_ref[hp] = lax.dot_general(keys_ref[hp], q[:, hp * PEER_HALF:(hp + 1) * PEER_HALF], (((1,), (1,)), ((), ())),
                                     preferred_element_type=jnp.float32)


def _mid_proj(x2d, y_r, y_g, w_out_b, ln2_w, wq_b, keys_b):
    t = x2d.shape[0]
    tok = lambda w: pl.BlockSpec((MID_TM, w), lambda i: (i, 0))
    nhp = 2 * PEER_HEADS
    return pl.pallas_call(
        _mid_kernel,
        grid=(t // MID_TM,),
        in_specs=[tok(D_MODEL), tok(RET_WIDTH), tok(GDN_WIDTH),
                  pl.BlockSpec((D_MODEL, D_MODEL), lambda i: (0, 0)),
                  pl.BlockSpec((1, D_MODEL), lambda i: (0, 0)),
                  pl.BlockSpec(wq_b.shape, lambda i: (0, 0)),
                  pl.BlockSpec((nhp, PEER_KEYS, PEER_HALF), lambda i: (0, 0, 0))],
        out_specs=[tok(D_MODEL), tok(D_MODEL), pl.BlockSpec((nhp, PEER_KEYS, MID_TM), lambda i: (0, 0, i))],
        out_shape=[jax.ShapeDtypeStruct((t, D_MODEL), jnp.float32),
                   jax.ShapeDtypeStruct((t, D_MODEL), jnp.float32),
                   jax.ShapeDtypeStruct((nhp, PEER_KEYS, t), jnp.float32)],
        compiler_params=pltpu.CompilerParams(dimension_semantics=("arbitrary",), vmem_limit_bytes=VMEM_LIMIT),
        name="mid_proj",
    )(x2d, y_r, y_g, w_out_b, ln2_w.reshape(1, D_MODEL), wq_b, keys_b)


TOPK_TL = LANES
_PAIRS = [(a, b) for a in range(PEER_TOPK) for b in range(PEER_TOPK) if (a + 1) * (b + 1) <= PEER_TOPK]
_NPAIR_PAD = -(-len(_PAIRS) // SUBLANES) * SUBLANES
_NCAND = PEER_TOPK * PEER_TOPK


def _top16(s, n):
    iota = lax.broadcasted_iota(jnp.int32, s.shape, 0)
    vals, idxs = [], []
    for _ in range(PEER_TOPK):
        m = jnp.max(s, axis=0, keepdims=True)
        idx = jnp.min(jnp.where(s == m, iota, n), axis=0, keepdims=True)
        vals.append(m)
        idxs.append(idx)
        s = jnp.where(iota == idx, -jnp.inf, s)
    return jnp.concatenate(vals, axis=0), jnp.concatenate(idxs, axis=0)


def _topk_kernel(st_ref, ids_ref, gate_ref):
    tl = st_ref.shape[2]
    npad = _NPAIR_PAD - len(_PAIRS)
    riota = lax.broadcasted_iota(jnp.int32, (_NPAIR_PAD, tl), 0)
    pos = jnp.full((_NPAIR_PAD, tl), _NCAND, jnp.int32)
    for r, (a, b) in enumerate(_PAIRS):
        pos = jnp.where(riota == r, a * PEER_TOPK + b, pos)
    ids_rows, gate_rows = [], []
    for h in range(PEER_HEADS):
        s1, i1 = _top16(st_ref[2 * h], PEER_KEYS)
        s2, i2 = _top16(st_ref[2 * h + 1], PEER_KEYS)
        cand = jnp.concatenate([s1[a:a + 1, :] + s2[b:b + 1, :] for a, b in _PAIRS]
                               + [jnp.full((npad, tl), -jnp.inf, jnp.float32)], axis=0)
        cid = jnp.concatenate([i1[a:a + 1, :] * PEER_KEYS + i2[b:b + 1, :] for a, b in _PAIRS]
                              + [jnp.zeros((npad, tl), jnp.int32)], axis=0)
        best, sel = [], []
        for _ in range(PEER_TOPK):
            m = jnp.max(cand, axis=0, keepdims=True)
            p = jnp.min(jnp.where(cand == m, pos, _NCAND + 1), axis=0, keepdims=True)
            hit = pos == p
            best.append(m)
            sel.append(jnp.max(jnp.where(hit, cid, -1), axis=0, keepdims=True))
            cand = jnp.where(hit, -jnp.inf, cand)
        best = jnp.concatenate(best, axis=0)
        e = jnp.exp(best - best[0:1, :])
        gate_rows.append(e / jnp.sum(e, axis=0, keepdims=True))
        ids_rows.append(jnp.concatenate(sel, axis=0))
    ids_ref[...] = jnp.concatenate(ids_rows, axis=0).T
    gate_ref[...] = jnp.concatenate(gate_rows, axis=0).T


def _peer_topk(st):
    t = st.shape[2]
    return pl.pallas_call(
        _topk_kernel,
        grid=(t // TOPK_TL,),
        in_specs=[pl.BlockSpec((2 * PEER_HEADS, PEER_KEYS, TOPK_TL), lambda i: (0, 0, i))],
        out_specs=[pl.BlockSpec((TOPK_TL, PEER_NK), lambda i: (i, 0)),
                   pl.BlockSpec((TOPK_TL, PEER_NK), lambda i: (i, 0))],
        out_shape=[jax.ShapeDtypeStruct((t, PEER_NK), jnp.int32),
                   jax.ShapeDtypeStruct((t, PEER_NK), jnp.float32)],
        compiler_params=pltpu.CompilerParams(dimension_semantics=("arbitrary",)),
        name="peer_topk",
    )(st)


PACK_TE = 256


def _pack_kernel(u_ref, v_ref, o_ref):
    hi = lax.bitcast_convert_type(u_ref[...].astype(jnp.bfloat16).astype(jnp.float32), jnp.uint32)
    lo = lax.bitcast_convert_type(v_ref[...].astype(jnp.bfloat16).astype(jnp.float32), jnp.uint32)
    o_ref[...] = hi | (lo >> 16)


def _peer_pack(u_tab, v_tab):
    ne = u_tab.shape[0]
    blk = pl.BlockSpec((PACK_TE, D_MODEL), lambda i: (i, 0))
    return pl.pallas_call(
        _pack_kernel,
        grid=(ne // PACK_TE,),
        in_specs=[blk, blk],
        out_specs=blk,
        out_shape=jax.ShapeDtypeStruct((ne, D_MODEL), jnp.uint32),
        compiler_params=pltpu.CompilerParams(dimension_semantics=("arbitrary",)),
        name="peer_pack",
    )(u_tab, v_tab)


PG_TB = 32
PG_SB = SUBLANES
PG_NSUB = PG_TB // PG_SB
PG_ROWS = PG_SB * PEER_NK
PG_NCH = D_MODEL // LANES


def _pg_issue(ids_ref, base, uv_ref, buf, sem, slot):
    def body(j, _):
        for k in range(PEER_NK):
            e = ids_ref[base + j * PEER_NK + k]
            pltpu.make_async_copy(uv_ref.at[e], buf.at[slot, :, pl.ds(j * PEER_NK + k, 1), :], sem.at[slot]).start()
        return 0
    lax.fori_loop(0, PG_SB, body, 0)


def _peer_gather_kernel(ids_ref, idsn_ref, gate_ref, h2_ref, x1_ref, lnf_ref, after_ref, uv_ref, o_ref, buf, sem):
    del after_ref
    i = pl.program_id(0)
    n = pl.num_programs(0)

    @pl.when(i == 0)
    def _():
        _pg_issue(ids_ref, 0, uv_ref, buf, sem, 0)

    sub = lax.broadcasted_iota(jnp.int32, (PG_SB, PEER_NK), 0)
    lane_tok = lax.broadcasted_iota(jnp.int32, (PG_SB, PG_ROWS), 1) // PEER_NK
    sub_w = lax.broadcasted_iota(jnp.int32, (PG_SB, PG_ROWS), 0)
    for s in range(PG_NSUB):
        slot = s % 2
        nslot = (s + 1) % 2
        if s + 1 < PG_NSUB:
            _pg_issue(ids_ref, (s + 1) * PG_ROWS, uv_ref, buf, sem, nslot)
        else:
            @pl.when(i + 1 < n)
            def _():
                _pg_issue(idsn_ref, 0, uv_ref, buf, sem, nslot)
        pltpu.make_async_copy(buf.at[slot], buf.at[slot], sem.at[slot]).wait()
        rows = pl.ds(s * PG_SB, PG_SB)
        h8 = h2_ref[rows, :]
        words = jnp.concatenate([buf[slot, c] for c in range(PG_NCH)], axis=1)
        u = lax.bitcast_convert_type(words & jnp.uint32(0xFFFF0000), jnp.float32)
        v = lax.bitcast_convert_type(words << 16, jnp.float32)
        dt = lax.dot_general(h8, u, (((1,), (1,)), ((), ())), preferred_element_type=jnp.float32)
        d = jnp.zeros((PG_SB, PEER_NK), jnp.float32)
        for j in range(PG_SB):
            d = d + jnp.where(sub == j, dt[:, j * PEER_NK:(j + 1) * PEER_NK], 0.0)
        act = 0.5 * d * (1.0 + lax.erf(d * (2.0 ** -0.5)))
        w = gate_ref[rows, :] * act
        wbd = jnp.where(lane_tok == sub_w, jnp.concatenate([w] * PG_SB, axis=1), 0.0)
        y = x1_ref[rows, :] + jnp.dot(wbd, v, preferred_element_type=jnp.float32)
        ms = jnp.mean(y * y, axis=-1, keepdims=True)
        o_ref[rows, :] = y * lax.rsqrt(ms + NORM_EPS) * lnf_ref[...]


def _peer_gather(ids_flat, gate, h2, x1, lnf_w, uv4, tok0, ntok, after):
    assert tok0 % PG_TB == 0 and ntok % PG_TB == 0
    nsteps = ntok // PG_TB
    b0 = tok0 // PG_TB
    blk = PG_TB * PEER_NK
    tok = lambda w: pl.BlockSpec((PG_TB, w), lambda i: (b0 + i, 0))
    return pl.pallas_call(
        _peer_gather_kernel,
        grid=(nsteps,),
        in_specs=[
            pl.BlockSpec((blk,), lambda i: (b0 + i,), memory_space=pltpu.SMEM),
            pl.BlockSpec((blk,), lambda i: (b0 + jnp.minimum(i + 1, nsteps - 1),), memory_space=pltpu.SMEM),
            tok(PEER_NK), tok(D_MODEL), tok(D_MODEL),
            pl.BlockSpec((1, D_MODEL), lambda i: (0, 0)),
            pl.BlockSpec((SUBLANES, LANES), lambda i: (0, 0)),
            pl.BlockSpec(memory_space=pl.ANY),
        ],
        out_specs=pl.BlockSpec((PG_TB, D_MODEL), lambda i: (i, 0)),
        out_shape=jax.ShapeDtypeStruct((ntok, D_MODEL), jnp.float32),
        scratch_shapes=[pltpu.VMEM((2, PG_NCH, PG_ROWS, LANES), jnp.uint32), pltpu.SemaphoreType.DMA((2,))],
        compiler_params=pltpu.CompilerParams(dimension_semantics=("arbitrary",), vmem_limit_bytes=VMEM_LIMIT),
        name="peer_gather",
    )(ids_flat, ids_flat, gate, h2, x1, lnf_w.reshape(1, D_MODEL), after, uv4)


SC_CORES = 2
SC_SUBCORES = 16
SC_LANES = 16
SC_WORKERS = SC_CORES * SC_SUBCORES
SC_G = 32
SC_NG = PEER_NK // SC_G
SC_NCH = D_MODEL // SC_LANES
SC_HC = SC_NCH // 2


def _sc_pipeline(tab_hbm, ids_hbm, vec_hbm, out_hbm, ids_v, vec_v, rows_v, obuf, gsem, isem, osem, tok_base, tpt, compute):
    wid = lax.axis_index("s") * SC_CORES + lax.axis_index("c")
    tok0 = wid * tpt

    def gather(ts, g):
        return pltpu.make_async_copy(tab_hbm.at[ids_v.at[ts, g]], rows_v.at[g % 2], gsem.at[g % 2])

    def load_inputs(j, ts):
        return (pltpu.make_async_copy(ids_hbm.at[tok_base + j], ids_v.at[ts], isem),
                pltpu.make_async_copy(vec_hbm.at[tok_base + j], vec_v.at[ts], isem))

    def out_copy(j, ts):
        return pltpu.make_async_copy(obuf.at[ts], out_hbm.at[j], osem.at[ts])

    for cp in load_inputs(tok0, 0):
        cp.start()
    for cp in load_inputs(tok0, 0):
        cp.wait()
    gather(0, 0).start()
    gather(0, 1).start()

    @pl.loop(0, tpt, step=2)
    def _(it):
        for ts in range(2):
            j = tok0 + it + ts
            more = (it + ts + 1) < tpt

            @pl.when(more)
            def _():
                for cp in load_inputs(j + 1, 1 - ts):
                    cp.start()

            @pl.when((it + ts) >= 2)
            def _():
                out_copy(j - 2, ts).wait()

            for g in range(SC_NG):
                gather(ts, g).wait()
                compute(ts, g)
                if g + 2 < SC_NG:
                    gather(ts, g + 2).start()
                else:
                    if g + 2 == SC_NG:
                        @pl.when(more)
                        def _():
                            for cp in load_inputs(j + 1, 1 - ts):
                                cp.wait()

                    @pl.when(more)
                    def _():
                        gather(1 - ts, g + 2 - SC_NG).start()
            out_copy(j, ts).start()

    out_copy(tok0 + tpt - 2, 0).wait()
    out_copy(tok0 + tpt - 1, 1).wait()


def _sc_call(body, out_width, vec_width, tok_base, ntok):
    assert ntok % (2 * SC_WORKERS) == 0
    return functools.partial(
        pl.kernel,
        mesh=plsc.VectorSubcoreMesh(core_axis_name="c", subcore_axis_name="s"),
        compiler_params=pltpu.CompilerParams(needs_layout_passes=False),
        out_type=jax.ShapeDtypeStruct((ntok, out_width), jnp.float32),
        scratch_types=[
            pltpu.VMEM((2, SC_NG, SC_G), jnp.int32),
            pltpu.VMEM((2, vec_width), jnp.float32),
            pltpu.VMEM((2, SC_G, D_MODEL), jnp.float32),
            pltpu.VMEM((2, out_width), jnp.float32),
            pltpu.SemaphoreType.DMA((2,)),
            pltpu.SemaphoreType.DMA,
            pltpu.SemaphoreType.DMA((2,)),
        ],
    )(functools.partial(body, tok_base=tok_base, tpt=ntok // SC_WORKERS))


def _sc_dots_body(u_hbm, ids_hbm, h_hbm, out_hbm, ids_v, h_v, rows_v, pbuf, gsem, isem, osem, *, tok_base, tpt):
    def compute(ts, g):
        def chunk(ci, accs):
            hc = h_v[ts, pl.ds(ci * SC_LANES, SC_LANES)]
            return tuple(accs[r] + rows_v[g % 2, r, pl.ds(ci * SC_LANES, SC_LANES)] * hc for r in range(SC_G))
        zeros = tuple(jnp.zeros((SC_LANES,), jnp.float32) for _ in range(SC_G))
        accs = plsc.parallel_loop(0, SC_NCH, carry=zeros)(chunk)
        for r in range(SC_G):
            pbuf[ts, pl.ds((g * SC_G + r) * SC_LANES, SC_LANES)] = accs[r]

    _sc_pipeline(u_hbm, ids_hbm, h_hbm, out_hbm, ids_v, h_v, rows_v, pbuf, gsem, isem, osem, tok_base, tpt, compute)


def _sc_wsum_body(v_hbm, ids_hbm, w_hbm, out_hbm, ids_v, w_v, rows_v, obuf, gsem, isem, osem, *, tok_base, tpt):
    def compute(ts, g):
        for half in range(2):
            def row(r, accs):
                wr = plsc.load_gather(w_v.at[ts], [jnp.full((SC_LANES,), g * SC_G, jnp.int32) + r])
                return tuple(accs[c] + rows_v[g % 2, r, pl.ds((half * SC_HC + c) * SC_LANES, SC_LANES)] * wr
                             for c in range(SC_HC))
            zeros = tuple(jnp.zeros((SC_LANES,), jnp.float32) for _ in range(SC_HC))
            accs = plsc.parallel_loop(0, SC_G, carry=zeros)(row)
            for c in range(SC_HC):
                sl = pl.ds((half * SC_HC + c) * SC_LANES, SC_LANES)
                if g == 0:
                    obuf[ts, sl] = accs[c]
                else:
                    plsc.addupdate(obuf.at[ts, sl], accs[c])

    _sc_pipeline(v_hbm, ids_hbm, w_hbm, out_hbm, ids_v, w_v, rows_v, obuf, gsem, isem, osem, tok_base, tpt, compute)


SCT_TM = 256


def _peer_act_kernel(dp_ref, gate_ref, after_ref, w_ref):
    del after_ref
    lane_exp = lax.broadcasted_iota(jnp.int32, (PEER_NK * SC_LANES, PEER_NK), 0) // SC_LANES
    fold = (lane_exp == lax.broadcasted_iota(jnp.int32, (PEER_NK * SC_LANES, PEER_NK), 1)).astype(jnp.float32)
    d = jnp.dot(dp_ref[...], fold, preferred_element_type=jnp.float32)
    w_ref[...] = gate_ref[...] * (0.5 * d * (1.0 + lax.erf(d * (2.0 ** -0.5))))


def _peer_act(dpart, gate, tok0, after):
    n = dpart.shape[0]
    b0 = tok0 // SCT_TM
    return pl.pallas_call(
        _peer_act_kernel,
        grid=(n // SCT_TM,),
        in_specs=[pl.BlockSpec((SCT_TM, PEER_NK * SC_LANES), lambda i: (i, 0)),
                  pl.BlockSpec((SCT_TM, PEER_NK), lambda i: (b0 + i, 0)),
                  pl.BlockSpec((SUBLANES, LANES), lambda i: (0, 0))],
        out_specs=pl.BlockSpec((SCT_TM, PEER_NK), lambda i: (i, 0)),
        out_shape=jax.ShapeDtypeStruct((n, PEER_NK), jnp.float32),
        compiler_params=pltpu.CompilerParams(dimension_semantics=("arbitrary",)),
        name="peer_act",
    )(dpart, gate, after)


def _peer_final_kernel(x1_ref, p_ref, lnf_ref, o_ref):
    y = x1_ref[...] + p_ref[...]
    o_ref[...] = y * lax.rsqrt(jnp.mean(y * y, axis=-1, keepdims=True) + NORM_EPS) * lnf_ref[...]


def _peer_final(x1, peer, lnf_w, tok0):
    n = peer.shape[0]
    b0 = tok0 // SCT_TM
    return pl.pallas_call(
        _peer_final_kernel,
        grid=(n // SCT_TM,),
        in_specs=[pl.BlockSpec((SCT_TM, D_MODEL), lambda i: (b0 + i, 0)),
                  pl.BlockSpec((SCT_TM, D_MODEL), lambda i: (i, 0)),
                  pl.BlockSpec((1, D_MODEL), lambda i: (0, 0))],
        out_specs=pl.BlockSpec((SCT_TM, D_MODEL), lambda i: (i, 0)),
        out_shape=jax.ShapeDtypeStruct((n, D_MODEL), jnp.float32),
        compiler_params=pltpu.CompilerParams(dimension_semantics=("arbitrary",)),
        name="peer_final",
    )(x1, peer, lnf_w.reshape(1, D_MODEL))


PEER_TC_A = 3392
PEER_TC_B = 4288


def _peer_mlp(ids, gate, h2, x1, lnf_w, u_tab, v_tab, uv4):
    t = ids.shape[0]
    n_tc = PEER_TC_A + PEER_TC_B
    n_sc = t - n_tc
    ids_flat = ids.reshape(t * PEER_NK)
    ids_g = ids.reshape(t, SC_NG, SC_G)
    dots = _sc_call(_sc_dots_body, PEER_NK * SC_LANES, D_MODEL, n_tc, n_sc)(u_tab, ids_g, h2)
    out_a = _peer_gather(ids_flat, gate, h2, x1, lnf_w, uv4, 0, PEER_TC_A, gate[:SUBLANES])
    w_sc = _peer_act(dots, gate, n_tc, out_a[:SUBLANES, :LANES])
    peer_sc = _sc_call(_sc_wsum_body, D_MODEL, PEER_NK, 0, n_sc)(v_tab, ids_g[n_tc:], w_sc)
    out_b = _peer_gather(ids_flat, gate, h2, x1, lnf_w, uv4, PEER_TC_A, PEER_TC_B, w_sc[:SUBLANES])
    out_sc = _peer_final(x1, peer_sc, lnf_w, n_tc)
    return jnp.concatenate([out_a, out_b, out_sc], axis=0)


def kernel(x, ln1_w, w_in, ret_norm_w, conv_w, A_log, dt_bias, gdn_norm_w, w_out, ln2_w, peer_wq, peer_keys, peer_u, peer_v, lnf_w):
    b, t, d = x.shape
    assert b == 1 and d == D_MODEL and t % MID_TM == 0 and ln1_w.shape[0] == 1
    l = 0
    x2d = x.reshape(t, d)

    cols = _in_proj_cols()
    w_p = jnp.where((cols >= 0)[None, :], jnp.take(w_in[l], jnp.maximum(cols, 0), axis=1), 0.0).astype(jnp.bfloat16)
    uv4 = _peer_pack(peer_u[l], peer_v[l]).reshape(peer_u.shape[1], PG_NCH, 1, LANES)
    keys_b = peer_keys[l].reshape(2 * PEER_HEADS, PEER_KEYS, PEER_HALF).astype(jnp.bfloat16)
    cos4, sin4 = _rope_tables(t)

    rqk, rv, rg, gqkv, gz, gab = _in_proj(x2d, ln1_w[l], w_p)
    y_r = _retention(rqk, rv, rg, cos4, sin4, ret_norm_w[l])
    y_g = _gdn(gqkv, gab, gz, conv_w[l], A_log[l], dt_bias[l], gdn_norm_w[l])
    x1, h2, st = _mid_proj(x2d, y_r, y_g, w_out[l].astype(jnp.bfloat16), ln2_w[l], peer_wq[l].astype(jnp.bfloat16), keys_b)
    ids, gate = _peer_topk(st)
    out = _peer_mlp(ids, gate, h2, x1, lnf_w, peer_u[l], peer_v[l], uv4)
    return out.reshape(b, t, d)
```

```python
import math

import jax
import jax.numpy as jnp
import numpy as np
from jax import lax
import functools

from jax.experimental import pallas as pl
from jax.experimental.pallas import tpu as pltpu
from jax.experimental.pallas import tpu_sc as plsc

D_MODEL = 1024
RET_HEADS = 4
RET_QK_DIM = 64
RET_HALF = RET_QK_DIM // 2
RET_V_DIM = 128
RET_WIDTH = RET_HEADS * RET_V_DIM
GDN_HEADS = 4
GDN_HEAD_DIM = 128
GDN_WIDTH = GDN_HEADS * GDN_HEAD_DIM
CONV_K = 4
CONV_CH = 3 * GDN_WIDTH
CHUNK = 128
ROPE_BASE = 10000.0
NORM_EPS = 1e-6
PEER_HEADS = 8
PEER_KEYS = 128
PEER_HALF = 128
PEER_TOPK = 16
PEER_NK = PEER_HEADS * PEER_TOPK
LANES = 128
SUBLANES = 8
VMEM_LIMIT = 48 * 1024 * 1024


def _silu(x):
    return x * jax.nn.sigmoid(x)


IN_TM = 256
IN_WIDTHS = (2 * RET_HEADS * RET_QK_DIM, RET_WIDTH, RET_WIDTH, CONV_CH, GDN_WIDTH, LANES)


def _in_proj_cols():
    splits = (256, 256, 512, 512, 512, 512, 512, 512, 4, 4)
    rq, rk, rv, rg, gq, _, _, gz, ga, _ = np.cumsum((0,) + splits)[:10]

    def rotary_halves(base):
        first = [base + h * RET_QK_DIM + d for h in range(RET_HEADS) for d in range(RET_HALF)]
        second = [base + h * RET_QK_DIM + RET_HALF + d for h in range(RET_HEADS) for d in range(RET_HALF)]
        return first + second

    cols = rotary_halves(rq) + rotary_halves(rk)
    cols += list(range(rv, rv + RET_WIDTH)) + list(range(rg, rg + RET_WIDTH))
    cols += list(range(gq, gq + CONV_CH)) + list(range(gz, gz + GDN_WIDTH))
    cols += list(range(ga, ga + 2 * GDN_HEADS)) + [-1] * (LANES - 2 * GDN_HEADS)
    return np.asarray(cols, np.int32)


def _in_proj_kernel(x_ref, ln_ref, w_ref, *out_refs):
    x = x_ref[...]
    h = x * lax.rsqrt(jnp.mean(x * x, axis=-1, keepdims=True) + NORM_EPS) * ln_ref[...]
    p = jnp.dot(h.astype(jnp.bfloat16), w_ref[...], preferred_element_type=jnp.float32)
    off = 0
    for ref, width in zip(out_refs, IN_WIDTHS):
        ref[...] = p[:, off:off + width]
        off += width


def _in_proj(x2d, ln_w, w_p):
    t = x2d.shape[0]
    return pl.pallas_call(
        _in_proj_kernel,
        grid=(t // IN_TM,),
        in_specs=[pl.BlockSpec((IN_TM, D_MODEL), lambda i: (i, 0)),
                  pl.BlockSpec((1, D_MODEL), lambda i: (0, 0)),
                  pl.BlockSpec(w_p.shape, lambda i: (0, 0))],
        out_specs=[pl.BlockSpec((IN_TM, w), lambda i: (i, 0)) for w in IN_WIDTHS],
        out_shape=[jax.ShapeDtypeStruct((t, w), jnp.float32) for w in IN_WIDTHS],
        compiler_params=pltpu.CompilerParams(dimension_semantics=("arbitrary",), vmem_limit_bytes=VMEM_LIMIT),
        name="in_proj",
    )(x2d, ln_w.reshape(1, D_MODEL), w_p)


def _retention_kernel(rqk_ref, rv_ref, rg_ref, cos_ref, sin_ref, nw_ref, o_ref, state):
    c = CHUNK

    @pl.when(pl.program_id(0) == 0)
    def _():
        state[...] = jnp.zeros_like(state)

    cos, sin = cos_ref[...], sin_ref[...]
    qa, qb = rqk_ref[:, 0:128], rqk_ref[:, 128:256]
    ka, kb = rqk_ref[:, 256:384], rqk_ref[:, 384:512]
    q = jnp.concatenate([qa * cos - qb * sin, qa * sin + qb * cos], axis=1)
    k = jnp.concatenate([ka * cos - kb * sin, ka * sin + kb * cos], axis=1) * (RET_QK_DIM ** -0.5)
    lane_head = (lax.broadcasted_iota(jnp.int32, (1, 2 * LANES), 1) % LANES) // RET_HALF
    ri = lax.broadcasted_iota(jnp.int32, (c, c), 0)
    ci = lax.broadcasted_iota(jnp.int32, (c, c), 1)
    diff = (ri - ci).astype(jnp.float32)
    pos = lax.broadcasted_iota(jnp.int32, (c, 1), 0).astype(jnp.float32)
    for h in range(RET_HEADS):
        lg = math.log1p(-(2.0 ** (-5.0 - h)))
        head_lanes = lane_head == h
        qh = jnp.where(head_lanes, q, 0.0)
        kh = jnp.where(head_lanes, k, 0.0)
        sl = slice(h * RET_V_DIM, (h + 1) * RET_V_DIM)
        v = rv_ref[:, sl]
        decay = jnp.where(diff >= 0, jnp.exp(lg * jnp.maximum(diff, 0.0)), 0.0)
        scores = lax.dot_general(qh, k, (((1,), (1,)), ((), ())), preferred_element_type=jnp.float32) * decay
        intra = jnp.dot(scores, v, preferred_element_type=jnp.float32)
        s_prev = state[h]
        inter = jnp.dot(qh * jnp.exp(lg * (pos + 1.0)), s_prev, preferred_element_type=jnp.float32)
        o = intra + inter
        k_dec = kh * jnp.exp(lg * (c - 1.0 - pos))
        state[h] = s_prev * math.exp(lg * c) + lax.dot_general(k_dec, v, (((0,), (0,)), ((), ())),
                                                              preferred_element_type=jnp.float32)
        mu = jnp.mean(o, axis=-1, keepdims=True)
        var = jnp.mean(jnp.square(o - mu), axis=-1, keepdims=True)
        o_ref[:, sl] = (o - mu) * lax.rsqrt(var + NORM_EPS) * nw_ref[:, sl] * _silu(rg_ref[:, sl])


def _retention(rqk, rv, rg, cos4, sin4, norm_w):
    t = rqk.shape[0]
    c = CHUNK
    tok = lambda w: pl.BlockSpec((c, w), lambda i: (i, 0))
    return pl.pallas_call(
        _retention_kernel,
        grid=(t // c,),
        in_specs=[tok(512), tok(RET_WIDTH), tok(RET_WIDTH), tok(LANES), tok(LANES),
                  pl.BlockSpec((1, RET_WIDTH), lambda i: (0, 0))],
        out_specs=tok(RET_WIDTH),
        out_shape=jax.ShapeDtypeStruct((t, RET_WIDTH), jnp.float32),
        scratch_shapes=[pltpu.VMEM((RET_HEADS, 2 * LANES, RET_V_DIM), jnp.float32)],
        compiler_params=pltpu.CompilerParams(dimension_semantics=("arbitrary",)),
        name="retention",
    )(rqk, rv, rg, cos4, sin4, norm_w.reshape(1, RET_WIDTH))


def _rope_tables(t):
    inv = ROPE_BASE ** (-jnp.arange(RET_HALF, dtype=jnp.float32) / RET_HALF)
    ang = jnp.arange(t, dtype=jnp.float32)[:, None] * inv[None, :]
    return jnp.tile(jnp.cos(ang), (1, RET_HEADS)), jnp.tile(jnp.sin(ang), (1, RET_HEADS))


GDN_CG = 2
GDN_UNITS = GDN_CG * GDN_HEADS


def _gdn_kernel(cur_ref, halo_ref, gab_ref, gz_ref, cw_ref, alog_ref, dtb_ref, nw_ref, o_ref,
                state, a_s, x_s, xa_s, rhs_s, attn_s, qd_s, ks_s):
    c = CHUNK
    n = GDN_CG * c
    d = GDN_HEAD_DIM
    i = pl.program_id(0)

    @pl.when(i == 0)
    def _():
        state[...] = jnp.zeros_like(state)

    halo = jnp.where(i > 0, halo_ref[...], 0.0)
    xe = jnp.concatenate([halo, cur_ref[...]], axis=0)
    first = SUBLANES - (CONV_K - 1)
    y = cw_ref[0:1, :] * xe[first:first + n]
    for j in range(1, CONV_K):
        y = y + cw_ref[j:j + 1, :] * xe[first + j:first + j + n]
    qkv = _silu(y)

    gab = gab_ref[...]
    sp_in = gab + dtb_ref[...]
    softplus = jnp.maximum(sp_in, 0.0) + jnp.log1p(jnp.exp(-jnp.abs(sp_in)))
    g_all = -jnp.exp(alog_ref[...]) * softplus
    beta_all = jax.nn.sigmoid(gab)
    ri = lax.broadcasted_iota(jnp.int32, (c, c), 0)
    ci = lax.broadcasted_iota(jnp.int32, (c, c), 1)
    causal = ri >= ci
    strict = ri > ci
    eye = (ri == ci).astype(jnp.float32)
    ltri = causal.astype(jnp.float32)
    utri = (ri <= ci).astype(jnp.float32)
    keep = {}
    for cc in range(GDN_CG):
        rows = slice(cc * c, (cc + 1) * c)
        g_c = g_all[rows]
        gc_all = jnp.dot(ltri, g_c, preferred_element_type=jnp.float32)
        gc_rows = jnp.dot(g_c.T, utri, preferred_element_type=jnp.float32)
        for h in range(GDN_HEADS):
            un = cc * GDN_HEADS + h
            q = qkv[rows, h * d:(h + 1) * d]
            k = qkv[rows, GDN_WIDTH + h * d:GDN_WIDTH + (h + 1) * d]
            v = qkv[rows, 2 * GDN_WIDTH + h * d:2 * GDN_WIDTH + (h + 1) * d]
            q = q * lax.rsqrt(jnp.sum(q * q, axis=-1, keepdims=True) + NORM_EPS) * (d ** -0.5)
            k = k * lax.rsqrt(jnp.sum(k * k, axis=-1, keepdims=True) + NORM_EPS)
            gc = gc_all[:, h:h + 1]
            gr = gc_rows[h:h + 1, :]
            beta = beta_all[rows, GDN_HEADS + h:GDN_HEADS + h + 1]
            lmask = jnp.where(causal, jnp.exp(jnp.where(causal, gc - gr, 0.0)), 0.0)
            kb = k * beta
            a = jnp.where(strict, lax.dot_general(kb, k, (((1,), (1,)), ((), ())), preferred_element_type=jnp.float32) * lmask, 0.0)
            a_s[un] = a
            x_s[un] = eye - jnp.where(ri // 2 == ci // 2, a, 0.0)
            egc = jnp.exp(gc)
            rhs_s[un] = jnp.concatenate([v * beta, kb * egc], axis=1)
            attn_s[un] = lax.dot_general(q, k, (((1,), (1,)), ((), ())), preferred_element_type=jnp.float32) * lmask
            glast = gc_all[c - 1:c, h:h + 1]
            qd_s[un] = q * egc
            ks_s[un] = k * jnp.exp(glast - gc)
            keep[un] = jnp.exp(glast)
    b = 2
    while b < c:
        off = (ri // (2 * b) == ci // (2 * b)) & (ri % (2 * b) >= b) & (ci % (2 * b) < b)
        for un in range(GDN_UNITS):
            xa_s[un] = jnp.dot(x_s[un], jnp.where(off, a_s[un], 0.0), preferred_element_type=jnp.float32)
        for un in range(GDN_UNITS):
            x_s[un] = x_s[un] - jnp.dot(xa_s[un], x_s[un], preferred_element_type=jnp.float32)
        b *= 2
    for un in range(GDN_UNITS):
        rhs_s[un] = jnp.dot(x_s[un], rhs_s[un], preferred_element_type=jnp.float32)
    for cc in range(GDN_CG):
        rows = slice(cc * c, (cc + 1) * c)
        for h in range(GDN_HEADS):
            un = cc * GDN_HEADS + h
            sl = slice(h * d, (h + 1) * d)
            s = state[h]
            v_new = rhs_s[un, :, :d] - jnp.dot(rhs_s[un, :, d:], s, preferred_element_type=jnp.float32)
            o = jnp.dot(qd_s[un], s, preferred_element_type=jnp.float32) + jnp.dot(attn_s[un], v_new, preferred_element_type=jnp.float32)
            state[h] = s * keep[un] + lax.dot_general(ks_s[un], v_new, (((0,), (0,)), ((), ())),
                                                      preferred_element_type=jnp.float32)
            on = o * lax.rsqrt(jnp.mean(o * o, axis=-1, keepdims=True) + NORM_EPS) * nw_ref[...]
            o_ref[rows, sl] = on * _silu(gz_ref[rows, sl])


def _gdn(gqkv, gab, gz, conv_w, a_log, dt_bias, norm_w):
    t = gqkv.shape[0]
    n = GDN_CG * CHUNK
    d = GDN_HEAD_DIM
    lane_pad = lambda v: jnp.zeros((1, LANES), jnp.float32).at[0, :GDN_HEADS].set(v)
    tok = lambda w: pl.BlockSpec((n, w), lambda i: (i, 0))
    const = lambda shape: pl.BlockSpec(shape, lambda i: (0, 0))
    unit = pltpu.VMEM((GDN_UNITS, CHUNK, d), jnp.float32)
    return pl.pallas_call(
        _gdn_kernel,
        grid=(t // n,),
        in_specs=[tok(CONV_CH),
                  pl.BlockSpec((SUBLANES, CONV_CH), lambda i: (jnp.maximum(i * (n // SUBLANES) - 1, 0), 0)),
                  tok(LANES), tok(GDN_WIDTH), const((CONV_K, CONV_CH)), const((1, LANES)), const((1, LANES)),
                  const((1, d))],
        out_specs=tok(GDN_WIDTH),
        out_shape=jax.ShapeDtypeStruct((t, GDN_WIDTH), jnp.float32),
        scratch_shapes=[pltpu.VMEM((GDN_HEADS, d, d), jnp.float32), unit, unit, unit,
                        pltpu.VMEM((GDN_UNITS, CHUNK, 2 * d), jnp.float32), unit, unit, unit],
        compiler_params=pltpu.CompilerParams(dimension_semantics=("arbitrary",)),
        name="gdn",
    )(gqkv, gqkv, gab, gz, conv_w, lane_pad(a_log), lane_pad(dt_bias), norm_w.reshape(1, d))


MID_TM = 256


def _mid_kernel(x_ref, yr_ref, yg_ref, wo_ref, ln_ref, wq_ref, keys_ref, x1_ref, h2_ref, st_ref):
    mixed = jnp.concatenate([yr_ref[...], yg_ref[...]], axis=1).astype(jnp.bfloat16)
    x1 = x_ref[...] + jnp.dot(mixed, wo_ref[...], preferred_element_type=jnp.float32)
    x1_ref[...] = x1
    h2 = x1 * lax.rsqrt(jnp.mean(x1 * x1, axis=-1, keepdims=True) + NORM_EPS) * ln_ref[...]
    h2_ref[...] = h2
    q = jnp.dot(h2.astype(jnp.bfloat16), wq_ref[...], preferred_element_type=jnp.float32).astype(jnp.bfloat16)
    for hp in range(2 * PEER_HEADS):
        st_ref[hp] = lax.dot_general(keys_ref[hp], q[:, hp * PEER_HALF:(hp + 1) * PEER_HALF], (((1,), (1,)), ((), ())),
                                     preferred_element_type=jnp.float32)


def _mid_proj(x2d, y_r, y_g, w_out_b, ln2_w, wq_b, keys_b, tok0, t):
    assert tok0 % MID_TM == 0 and t % MID_TM == 0
    b0 = tok0 // MID_TM
    src = lambda w: pl.BlockSpec((MID_TM, w), lambda i: (b0 + i, 0))
    tok = lambda w: pl.BlockSpec((MID_TM, w), lambda i: (i, 0))
    nhp = 2 * PEER_HEADS
    return pl.pallas_call(
        _mid_kernel,
        grid=(t // MID_TM,),
        in_specs=[src(D_MODEL), src(RET_WIDTH), src(GDN_WIDTH),
                  pl.BlockSpec((D_MODEL, D_MODEL), lambda i: (0, 0)),
                  pl.BlockSpec((1, D_MODEL), lambda i: (0, 0)),
                  pl.BlockSpec(wq_b.shape, lambda i: (0, 0)),
                  pl.BlockSpec((nhp, PEER_KEYS, PEER_HALF), lambda i: (0, 0, 0))],
        out_specs=[tok(D_MODEL), tok(D_MODEL), pl.BlockSpec((nhp, PEER_KEYS, MID_TM), lambda i: (0, 0, i))],
        out_shape=[jax.ShapeDtypeStruct((t, D_MODEL), jnp.float32),
                   jax.ShapeDtypeStruct((t, D_MODEL), jnp.float32),
                   jax.ShapeDtypeStruct((nhp, PEER_KEYS, t), jnp.float32)],
        compiler_params=pltpu.CompilerParams(dimension_semantics=("arbitrary",), vmem_limit_bytes=VMEM_LIMIT),
        name="mid_proj",
    )(x2d, y_r, y_g, w_out_b, ln2_w.reshape(1, D_MODEL), wq_b, keys_b)


TOPK_TL = LANES
_PAIRS = [(a, b) for a in range(PEER_TOPK) for b in range(PEER_TOPK) if (a + 1) * (b + 1) <= PEER_TOPK]
_NPAIR_PAD = -(-len(_PAIRS) // SUBLANES) * SUBLANES
_NCAND = PEER_TOPK * PEER_TOPK


def _top16(s, n):
    iota = lax.broadcasted_iota(jnp.int32, s.shape, 0)
    vals, idxs = [], []
    for _ in range(PEER_TOPK):
        m = jnp.max(s, axis=0, keepdims=True)
        idx = jnp.min(jnp.where(s == m, iota, n), axis=0, keepdims=True)
        vals.append(m)
        idxs.append(idx)
        s = jnp.where(iota == idx, -jnp.inf, s)
    return jnp.concatenate(vals, axis=0), jnp.concatenate(idxs, axis=0)


def _topk_kernel(st_ref, ids_ref, gate_ref):
    tl = st_ref.shape[2]
    npad = _NPAIR_PAD - len(_PAIRS)
    riota = lax.broadcasted_iota(jnp.int32, (_NPAIR_PAD, tl), 0)
    pos = jnp.full((_NPAIR_PAD, tl), _NCAND, jnp.int32)
    for r, (a, b) in enumerate(_PAIRS):
        pos = jnp.where(riota == r, a * PEER_TOPK + b, pos)
    ids_rows, gate_rows = [], []
    for h in range(PEER_HEADS):
        s1, i1 = _top16(st_ref[2 * h], PEER_KEYS)
        s2, i2 = _top16(st_ref[2 * h + 1], PEER_KEYS)
        cand = jnp.concatenate([s1[a:a + 1, :] + s2[b:b + 1, :] for a, b in _PAIRS]
                               + [jnp.full((npad, tl), -jnp.inf, jnp.float32)], axis=0)
        cid = jnp.concatenate([i1[a:a + 1, :] * PEER_KEYS + i2[b:b + 1, :] for a, b in _PAIRS]
                              + [jnp.zeros((npad, tl), jnp.int32)], axis=0)
        best, sel = [], []
        for _ in range(PEER_TOPK):
            m = jnp.max(cand, axis=0, keepdims=True)
            p = jnp.min(jnp.where(cand == m, pos, _NCAND + 1), axis=0, keepdims=True)
            hit = pos == p
            best.append(m)
            sel.append(jnp.max(jnp.where(hit, cid, -1), axis=0, keepdims=True))
            cand = jnp.where(hit, -jnp.inf, cand)
        best = jnp.concatenate(best, axis=0)
        e = jnp.exp(best - best[0:1, :])
        gate_rows.append(e / jnp.sum(e, axis=0, keepdims=True))
        ids_rows.append(jnp.concatenate(sel, axis=0))
    ids_ref[...] = jnp.concatenate(ids_rows, axis=0).T
    gate_ref[...] = jnp.concatenate(gate_rows, axis=0).T


def _peer_topk(st):
    t = st.shape[2]
    return pl.pallas_call(
        _topk_kernel,
        grid=(t // TOPK_TL,),
        in_specs=[pl.BlockSpec((2 * PEER_HEADS, PEER_KEYS, TOPK_TL), lambda i: (0, 0, i))],
        out_specs=[pl.BlockSpec((TOPK_TL, PEER_NK), lambda i: (i, 0)),
                   pl.BlockSpec((TOPK_TL, PEER_NK), lambda i: (i, 0))],
        out_shape=[jax.ShapeDtypeStruct((t, PEER_NK), jnp.int32),
                   jax.ShapeDtypeStruct((t, PEER_NK), jnp.float32)],
        compiler_params=pltpu.CompilerParams(dimension_semantics=("arbitrary",)),
        name="peer_topk",
    )(st)


PACK_TE = 256


def _pack_kernel(u_ref, v_ref, o_ref):
    hi = lax.bitcast_convert_type(u_ref[...].astype(jnp.bfloat16).astype(jnp.float32), jnp.uint32)
    lo = lax.bitcast_convert_type(v_ref[...].astype(jnp.bfloat16).astype(jnp.float32), jnp.uint32)
    o_ref[...] = hi | (lo >> 16)


def _peer_pack(u_tab, v_tab):
    ne = u_tab.shape[0]
    blk = pl.BlockSpec((PACK_TE, D_MODEL), lambda i: (i, 0))
    return pl.pallas_call(
        _pack_kernel,
        grid=(ne // PACK_TE,),
        in_specs=[blk, blk],
        out_specs=blk,
        out_shape=jax.ShapeDtypeStruct((ne, D_MODEL), jnp.uint32),
        compiler_params=pltpu.CompilerParams(dimension_semantics=("arbitrary",)),
        name="peer_pack",
    )(u_tab, v_tab)


PG_TB = 32
PG_SB = SUBLANES
PG_NSUB = PG_TB // PG_SB
PG_ROWS = PG_SB * PEER_NK
PG_NCH = D_MODEL // LANES


def _pg_issue(ids_ref, base, uv_ref, buf, sem, slot):
    def body(j, _):
        for k in range(PEER_NK):
            e = ids_ref[base + j * PEER_NK + k]
            pltpu.make_async_copy(uv_ref.at[e], buf.at[slot, :, pl.ds(j * PEER_NK + k, 1), :], sem.at[slot]).start()
        return 0
    lax.fori_loop(0, PG_SB, body, 0)


def _peer_gather_kernel(ids_ref, idsn_ref, gate_ref, h2_ref, x1_ref, lnf_ref, after_ref, uv_ref, o_ref, buf, sem):
    del after_ref
    i = pl.program_id(0)
    n = pl.num_programs(0)

    @pl.when(i == 0)
    def _():
        _pg_issue(ids_ref, 0, uv_ref, buf, sem, 0)

    sub = lax.broadcasted_iota(jnp.int32, (PG_SB, PEER_NK), 0)
    lane_tok = lax.broadcasted_iota(jnp.int32, (PG_SB, PG_ROWS), 1) // PEER_NK
    sub_w = lax.broadcasted_iota(jnp.int32, (PG_SB, PG_ROWS), 0)
    for s in range(PG_NSUB):
        slot = s % 2
        nslot = (s + 1) % 2
        if s + 1 < PG_NSUB:
            _pg_issue(ids_ref, (s + 1) * PG_ROWS, uv_ref, buf, sem, nslot)
        else:
            @pl.when(i + 1 < n)
            def _():
                _pg_issue(idsn_ref, 0, uv_ref, buf, sem, nslot)
        pltpu.make_async_copy(buf.at[slot], buf.at[slot], sem.at[slot]).wait()
        rows = pl.ds(s * PG_SB, PG_SB)
        h8 = h2_ref[rows, :]
        words = jnp.concatenate([buf[slot, c] for c in range(PG_NCH)], axis=1)
        u = lax.bitcast_convert_type(words & jnp.uint32(0xFFFF0000), jnp.float32)
        v = lax.bitcast_convert_type(words << 16, jnp.float32)
        dt = lax.dot_general(h8, u, (((1,), (1,)), ((), ())), preferred_element_type=jnp.float32)
        d = jnp.zeros((PG_SB, PEER_NK), jnp.float32)
        for j in range(PG_SB):
            d = d + jnp.where(sub == j, dt[:, j * PEER_NK:(j + 1) * PEER_NK], 0.0)
        act = 0.5 * d * (1.0 + lax.erf(d * (2.0 ** -0.5)))
        w = gate_ref[rows, :] * act
        wbd = jnp.where(lane_tok == sub_w, jnp.concatenate([w] * PG_SB, axis=1), 0.0)
        y = x1_ref[rows, :] + jnp.dot(wbd, v, preferred_element_type=jnp.float32)
        ms = jnp.mean(y * y, axis=-1, keepdims=True)
        o_ref[rows, :] = y * lax.rsqrt(ms + NORM_EPS) * lnf_ref[...]


def _peer_gather(ids_flat, gate, h2, x1, lnf_w, uv4, tok0, ntok, after):
    assert tok0 % PG_TB == 0 and ntok % PG_TB == 0
    nsteps = ntok // PG_TB
    b0 = tok0 // PG_TB
    blk = PG_TB * PEER_NK
    tok = lambda w: pl.BlockSpec((PG_TB, w), lambda i: (b0 + i, 0))
    return pl.pallas_call(
        _peer_gather_kernel,
        grid=(nsteps,),
        in_specs=[
            pl.BlockSpec((blk,), lambda i: (b0 + i,), memory_space=pltpu.SMEM),
            pl.BlockSpec((blk,), lambda i: (b0 + jnp.minimum(i + 1, nsteps - 1),), memory_space=pltpu.SMEM),
            tok(PEER_NK), tok(D_MODEL), tok(D_MODEL),
            pl.BlockSpec((1, D_MODEL), lambda i: (0, 0)),
            pl.BlockSpec((SUBLANES, LANES), lambda i: (0, 0)),
            pl.BlockSpec(memory_space=pl.ANY),
        ],
        out_specs=pl.BlockSpec((PG_TB, D_MODEL), lambda i: (i, 0)),
        out_shape=jax.ShapeDtypeStruct((ntok, D_MODEL), jnp.float32),
        scratch_shapes=[pltpu.VMEM((2, PG_NCH, PG_ROWS, LANES), jnp.uint32), pltpu.SemaphoreType.DMA((2,))],
        compiler_params=pltpu.CompilerParams(dimension_semantics=("arbitrary",), vmem_limit_bytes=VMEM_LIMIT),
        name="peer_gather",
    )(ids_flat, ids_flat, gate, h2, x1, lnf_w.reshape(1, D_MODEL), after, uv4)


SC_CORES = 2
SC_SUBCORES = 16
SC_LANES = 16
SC_WORKERS = SC_CORES * SC_SUBCORES
SC_G = 32
SC_NG = PEER_NK // SC_G
SC_NCH = D_MODEL // SC_LANES
SC_HC = SC_NCH // 2


def _sc_pipeline(tab_hbm, ids_hbm, vec_hbm, out_hbm, ids_v, vec_v, rows_v, obuf, gsem, isem, osem, tok_base, tpt, compute):
    wid = lax.axis_index("s") * SC_CORES + lax.axis_index("c")
    tok0 = wid * tpt

    def gather(ts, g):
        return pltpu.make_async_copy(tab_hbm.at[ids_v.at[ts, g]], rows_v.at[g % 2], gsem.at[g % 2])

    def load_inputs(j, ts):
        return (pltpu.make_async_copy(ids_hbm.at[tok_base + j], ids_v.at[ts], isem),
                pltpu.make_async_copy(vec_hbm.at[tok_base + j], vec_v.at[ts], isem))

    def out_copy(j, ts):
        return pltpu.make_async_copy(obuf.at[ts], out_hbm.at[j], osem.at[ts])

    for cp in load_inputs(tok0, 0):
        cp.start()
    for cp in load_inputs(tok0, 0):
        cp.wait()
    gather(0, 0).start()
    gather(0, 1).start()

    @pl.loop(0, tpt, step=2)
    def _(it):
        for ts in range(2):
            j = tok0 + it + ts
            more = (it + ts + 1) < tpt

            @pl.when(more)
            def _():
                for cp in load_inputs(j + 1, 1 - ts):
                    cp.start()

            @pl.when((it + ts) >= 2)
            def _():
                out_copy(j - 2, ts).wait()

            for g in range(SC_NG):
                gather(ts, g).wait()
                compute(ts, g)
                if g + 2 < SC_NG:
                    gather(ts, g + 2).start()
                else:
                    if g + 2 == SC_NG:
                        @pl.when(more)
                        def _():
                            for cp in load_inputs(j + 1, 1 - ts):
                                cp.wait()

                    @pl.when(more)
                    def _():
                        gather(1 - ts, g + 2 - SC_NG).start()
            out_copy(j, ts).start()

    out_copy(tok0 + tpt - 2, 0).wait()
    out_copy(tok0 + tpt - 1, 1).wait()


def _sc_call(body, out_width, vec_width, tok_base, ntok):
    assert ntok % (2 * SC_WORKERS) == 0
    return functools.partial(
        pl.kernel,
        mesh=plsc.VectorSubcoreMesh(core_axis_name="c", subcore_axis_name="s"),
        compiler_params=pltpu.CompilerParams(needs_layout_passes=False),
        out_type=jax.ShapeDtypeStruct((ntok, out_width), jnp.float32),
        scratch_types=[
            pltpu.VMEM((2, SC_NG, SC_G), jnp.int32),
            pltpu.VMEM((2, vec_width), jnp.float32),
            pltpu.VMEM((2, SC_G, D_MODEL), jnp.float32),
            pltpu.VMEM((2, out_width), jnp.float32),
            pltpu.SemaphoreType.DMA((2,)),
            pltpu.SemaphoreType.DMA,
            pltpu.SemaphoreType.DMA((2,)),
        ],
    )(functools.partial(body, tok_base=tok_base, tpt=ntok // SC_WORKERS))


def _sc_dots_body(u_hbm, ids_hbm, h_hbm, out_hbm, ids_v, h_v, rows_v, pbuf, gsem, isem, osem, *, tok_base, tpt):
    def compute(ts, g):
        def chunk(ci, accs):
            hc = h_v[ts, pl.ds(ci * SC_LANES, SC_LANES)]
            return tuple(accs[r] + rows_v[g % 2, r, pl.ds(ci * SC_LANES, SC_LANES)] * hc for r in range(SC_G))
        zeros = tuple(jnp.zeros((SC_LANES,), jnp.float32) for _ in range(SC_G))
        accs = plsc.parallel_loop(0, SC_NCH, carry=zeros)(chunk)
        for r in range(SC_G):
            pbuf[ts, pl.ds((g * SC_G + r) * SC_LANES, SC_LANES)] = accs[r]

    _sc_pipeline(u_hbm, ids_hbm, h_hbm, out_hbm, ids_v, h_v, rows_v, pbuf, gsem, isem, osem, tok_base, tpt, compute)


def _sc_wsum_body(v_hbm, ids_hbm, w_hbm, out_hbm, ids_v, w_v, rows_v, obuf, gsem, isem, osem, *, tok_base, tpt):
    def compute(ts, g):
        for half in range(2):
            def row(r, accs):
                wr = plsc.load_gather(w_v.at[ts], [jnp.full((SC_LANES,), g * SC_G, jnp.int32) + r])
                return tuple(accs[c] + rows_v[g % 2, r, pl.ds((half * SC_HC + c) * SC_LANES, SC_LANES)] * wr
                             for c in range(SC_HC))
            zeros = tuple(jnp.zeros((SC_LANES,), jnp.float32) for _ in range(SC_HC))
            accs = plsc.parallel_loop(0, SC_G, carry=zeros)(row)
            for c in range(SC_HC):
                sl = pl.ds((half * SC_HC + c) * SC_LANES, SC_LANES)
                if g == 0:
                    obuf[ts, sl] = accs[c]
                else:
                    plsc.addupdate(obuf.at[ts, sl], accs[c])

    _sc_pipeline(v_hbm, ids_hbm, w_hbm, out_hbm, ids_v, w_v, rows_v, obuf, gsem, isem, osem, tok_base, tpt, compute)


SCT_TM = 256


def _peer_act_kernel(dp_ref, gate_ref, after_ref, w_ref):
    del after_ref
    lane_exp = lax.broadcasted_iota(jnp.int32, (PEER_NK * SC_LANES, PEER_NK), 0) // SC_LANES
    fold = (lane_exp == lax.broadcasted_iota(jnp.int32, (PEER_NK * SC_LANES, PEER_NK), 1)).astype(jnp.float32)
    d = jnp.dot(dp_ref[...], fold, preferred_element_type=jnp.float32)
    w_ref[...] = gate_ref[...] * (0.5 * d * (1.0 + lax.erf(d * (2.0 ** -0.5))))


def _peer_act(dpart, gate, tok0, after):
    n = dpart.shape[0]
    b0 = tok0 // SCT_TM
    return pl.pallas_call(
        _peer_act_kernel,
        grid=(n // SCT_TM,),
        in_specs=[pl.BlockSpec((SCT_TM, PEER_NK * SC_LANES), lambda i: (i, 0)),
                  pl.BlockSpec((SCT_TM, PEER_NK), lambda i: (b0 + i, 0)),
                  pl.BlockSpec((SUBLANES, LANES), lambda i: (0, 0))],
        out_specs=pl.BlockSpec((SCT_TM, PEER_NK), lambda i: (i, 0)),
        out_shape=jax.ShapeDtypeStruct((n, PEER_NK), jnp.float32),
        compiler_params=pltpu.CompilerParams(dimension_semantics=("arbitrary",)),
        name="peer_act",
    )(dpart, gate, after)


def _peer_final_kernel(x1_ref, p_ref, lnf_ref, o_ref):
    y = x1_ref[...] + p_ref[...]
    o_ref[...] = y * lax.rsqrt(jnp.mean(y * y, axis=-1, keepdims=True) + NORM_EPS) * lnf_ref[...]


def _peer_final(x1, peer, lnf_w, tok0):
    n = peer.shape[0]
    b0 = tok0 // SCT_TM
    return pl.pallas_call(
        _peer_final_kernel,
        grid=(n // SCT_TM,),
        in_specs=[pl.BlockSpec((SCT_TM, D_MODEL), lambda i: (b0 + i, 0)),
                  pl.BlockSpec((SCT_TM, D_MODEL), lambda i: (i, 0)),
                  pl.BlockSpec((1, D_MODEL), lambda i: (0, 0))],
        out_specs=pl.BlockSpec((SCT_TM, D_MODEL), lambda i: (i, 0)),
        out_shape=jax.ShapeDtypeStruct((n, D_MODEL), jnp.float32),
        compiler_params=pltpu.CompilerParams(dimension_semantics=("arbitrary",)),
        name="peer_final",
    )(x1, peer, lnf_w.reshape(1, D_MODEL))


PEER_TC_A = 2720
PEER_TC_B = 4448


def _peer_mixer(t, route, lnf_w, u_tab, v_tab):
    n_tc = PEER_TC_A + PEER_TC_B
    n_sc = t - n_tc
    x1_s, h2_s, ids_s, gate_s = route(n_tc, n_sc)
    ids_g = ids_s.reshape(n_sc, SC_NG, SC_G)
    dots = _sc_call(_sc_dots_body, PEER_NK * SC_LANES, D_MODEL, 0, n_sc)(u_tab, ids_g, h2_s)
    uv4 = _peer_pack(u_tab, v_tab).reshape(u_tab.shape[0], PG_NCH, 1, LANES)
    x1_t, h2_t, ids_t, gate_t = route(0, n_tc)
    ids_flat = ids_t.reshape(n_tc * PEER_NK)
    out_a = _peer_gather(ids_flat, gate_t, h2_t, x1_t, lnf_w, uv4, 0, PEER_TC_A, gate_t[:SUBLANES])
    w_sc = _peer_act(dots, gate_s, 0, out_a[:SUBLANES, :LANES])
    peer_sc = _sc_call(_sc_wsum_body, D_MODEL, PEER_NK, 0, n_sc)(v_tab, ids_g, w_sc)
    out_b = _peer_gather(ids_flat, gate_t, h2_t, x1_t, lnf_w, uv4, PEER_TC_A, PEER_TC_B, w_sc[:SUBLANES])
    out_sc = _peer_final(x1_s, peer_sc, lnf_w, 0)
    return jnp.concatenate([out_a, out_b, out_sc], axis=0)


def kernel(x, ln1_w, w_in, ret_norm_w, conv_w, A_log, dt_bias, gdn_norm_w, w_out, ln2_w, peer_wq, peer_keys, peer_u, peer_v, lnf_w):
    b, t, d = x.shape
    assert b == 1 and d == D_MODEL and t % MID_TM == 0 and ln1_w.shape[0] == 1
    l = 0
    x2d = x.reshape(t, d)

    cols = _in_proj_cols()
    w_p = jnp.where((cols >= 0)[None, :], jnp.take(w_in[l], jnp.maximum(cols, 0), axis=1), 0.0).astype(jnp.bfloat16)
    keys_b = peer_keys[l].reshape(2 * PEER_HEADS, PEER_KEYS, PEER_HALF).astype(jnp.bfloat16)
    w_out_b, wq_b = w_out[l].astype(jnp.bfloat16), peer_wq[l].astype(jnp.bfloat16)
    cos4, sin4 = _rope_tables(t)

    rqk, rv, rg, gqkv, gz, gab = _in_proj(x2d, ln1_w[l], w_p)
    y_r = _retention(rqk, rv, rg, cos4, sin4, ret_norm_w[l])
    y_g = _gdn(gqkv, gab, gz, conv_w[l], A_log[l], dt_bias[l], gdn_norm_w[l])

    def route(tok0, n):
        x1, h2, st = _mid_proj(x2d, y_r, y_g, w_out_b, ln2_w[l], wq_b, keys_b, tok0, n)
        ids, gate = _peer_topk(st)
        return x1, h2, ids, gate

    out = _peer_mixer(t, route, lnf_w, peer_u[l], peer_v[l])
    return out.reshape(b, t, d)
```

```python
import math

import jax
import jax.numpy as jnp
import numpy as np
from jax import lax
import functools

from jax.experimental import pallas as pl
from jax.experimental.pallas import tpu as pltpu
from jax.experimental.pallas import tpu_sc as plsc

D_MODEL = 1024
RET_HEADS = 4
RET_QK_DIM = 64
RET_HALF = RET_QK_DIM // 2
RET_V_DIM = 128
RET_WIDTH = RET_HEADS * RET_V_DIM
GDN_HEADS = 4
GDN_HEAD_DIM = 128
GDN_WIDTH = GDN_HEADS * GDN_HEAD_DIM
CONV_K = 4
CONV_CH = 3 * GDN_WIDTH
CHUNK = 128
ROPE_BASE = 10000.0
NORM_EPS = 1e-6
PEER_HEADS = 8
PEER_KEYS = 128
PEER_HALF = 128
PEER_TOPK = 16
PEER_NK = PEER_HEADS * PEER_TOPK
LANES = 128
SUBLANES = 8
VMEM_LIMIT = 48 * 1024 * 1024


def _silu(x):
    return x * jax.nn.sigmoid(x)


IN_TM = 256
IN_WIDTHS = (2 * RET_HEADS * RET_QK_DIM, RET_WIDTH, RET_WIDTH, CONV_CH, GDN_WIDTH, LANES)


def _in_proj_cols():
    splits = (256, 256, 512, 512, 512, 512, 512, 512, 4, 4)
    rq, rk, rv, rg, gq, _, _, gz, ga, _ = np.cumsum((0,) + splits)[:10]

    def rotary_halves(base):
        first = [base + h * RET_QK_DIM + d for h in range(RET_HEADS) for d in range(RET_HALF)]
        second = [base + h * RET_QK_DIM + RET_HALF + d for h in range(RET_HEADS) for d in range(RET_HALF)]
        return first + second

    cols = rotary_halves(rq) + rotary_halves(rk)
    cols += list(range(rv, rv + RET_WIDTH)) + list(range(rg, rg + RET_WIDTH))
    cols += list(range(gq, gq + CONV_CH)) + list(range(gz, gz + GDN_WIDTH))
    cols += list(range(ga, ga + 2 * GDN_HEADS)) + [-1] * (LANES - 2 * GDN_HEADS)
    return np.asarray(cols, np.int32)


def _in_proj_kernel(x_ref, ln_ref, w_ref, *out_refs):
    x = x_ref[...]
    h = x * lax.rsqrt(jnp.mean(x * x, axis=-1, keepdims=True) + NORM_EPS) * ln_ref[...]
    p = jnp.dot(h.astype(jnp.bfloat16), w_ref[...], preferred_element_type=jnp.float32)
    off = 0
    for ref, width in zip(out_refs, IN_WIDTHS):
        ref[...] = p[:, off:off + width]
        off += width


def _in_proj(x2d, ln_w, w_p):
    t = x2d.shape[0]
    return pl.pallas_call(
        _in_proj_kernel,
        grid=(t // IN_TM,),
        in_specs=[pl.BlockSpec((IN_TM, D_MODEL), lambda i: (i, 0)),
                  pl.BlockSpec((1, D_MODEL), lambda i: (0, 0)),
                  pl.BlockSpec(w_p.shape, lambda i: (0, 0))],
        out_specs=[pl.BlockSpec((IN_TM, w), lambda i: (i, 0)) for w in IN_WIDTHS],
        out_shape=[jax.ShapeDtypeStruct((t, w), jnp.float32) for w in IN_WIDTHS],
        compiler_params=pltpu.CompilerParams(dimension_semantics=("arbitrary",), vmem_limit_bytes=VMEM_LIMIT),
        name="in_proj",
    )(x2d, ln_w.reshape(1, D_MODEL), w_p)


def _retention_kernel(rqk_ref, rv_ref, rg_ref, cos_ref, sin_ref, nw_ref, o_ref, state):
    c = CHUNK

    @pl.when(pl.program_id(0) == 0)
    def _():
        state[...] = jnp.zeros_like(state)

    cos, sin = cos_ref[...], sin_ref[...]
    qa, qb = rqk_ref[:, 0:128], rqk_ref[:, 128:256]
    ka, kb = rqk_ref[:, 256:384], rqk_ref[:, 384:512]
    q = jnp.concatenate([qa * cos - qb * sin, qa * sin + qb * cos], axis=1)
    k = jnp.concatenate([ka * cos - kb * sin, ka * sin + kb * cos], axis=1) * (RET_QK_DIM ** -0.5)
    lane_head = (lax.broadcasted_iota(jnp.int32, (1, 2 * LANES), 1) % LANES) // RET_HALF
    ri = lax.broadcasted_iota(jnp.int32, (c, c), 0)
    ci = lax.broadcasted_iota(jnp.int32, (c, c), 1)
    diff = (ri - ci).astype(jnp.float32)
    pos = lax.broadcasted_iota(jnp.int32, (c, 1), 0).astype(jnp.float32)
    for h in range(RET_HEADS):
        lg = math.log1p(-(2.0 ** (-5.0 - h)))
        head_lanes = lane_head == h
        qh = jnp.where(head_lanes, q, 0.0)
        kh = jnp.where(head_lanes, k, 0.0)
        sl = slice(h * RET_V_DIM, (h + 1) * RET_V_DIM)
        v = rv_ref[:, sl]
        decay = jnp.where(diff >= 0, jnp.exp(lg * jnp.maximum(diff, 0.0)), 0.0)
        scores = lax.dot_general(qh, k, (((1,), (1,)), ((), ())), preferred_element_type=jnp.float32) * decay
        intra = jnp.dot(scores, v, preferred_element_type=jnp.float32)
        s_prev = state[h]
        inter = jnp.dot(qh * jnp.exp(lg * (pos + 1.0)), s_prev, preferred_element_type=jnp.float32)
        o = intra + inter
        k_dec = kh * jnp.exp(lg * (c - 1.0 - pos))
        state[h] = s_prev * math.exp(lg * c) + lax.dot_general(k_dec, v, (((0,), (0,)), ((), ())),
                                                              preferred_element_type=jnp.float32)
        mu = jnp.mean(o, axis=-1, keepdims=True)
        var = jnp.mean(jnp.square(o - mu), axis=-1, keepdims=True)
        o_ref[:, sl] = (o - mu) * lax.rsqrt(var + NORM_EPS) * nw_ref[:, sl] * _silu(rg_ref[:, sl])


def _retention(rqk, rv, rg, cos4, sin4, norm_w):
    t = rqk.shape[0]
    c = CHUNK
    tok = lambda w: pl.BlockSpec((c, w), lambda i: (i, 0))
    return pl.pallas_call(
        _retention_kernel,
        grid=(t // c,),
        in_specs=[tok(512), tok(RET_WIDTH), tok(RET_WIDTH), tok(LANES), tok(LANES),
                  pl.BlockSpec((1, RET_WIDTH), lambda i: (0, 0))],
        out_specs=tok(RET_WIDTH),
        out_shape=jax.ShapeDtypeStruct((t, RET_WIDTH), jnp.float32),
        scratch_shapes=[pltpu.VMEM((RET_HEADS, 2 * LANES, RET_V_DIM), jnp.float32)],
        compiler_params=pltpu.CompilerParams(dimension_semantics=("arbitrary",)),
        name="retention",
    )(rqk, rv, rg, cos4, sin4, norm_w.reshape(1, RET_WIDTH))


def _rope_tables(t):
    inv = ROPE_BASE ** (-jnp.arange(RET_HALF, dtype=jnp.float32) / RET_HALF)
    ang = jnp.arange(t, dtype=jnp.float32)[:, None] * inv[None, :]
    return jnp.tile(jnp.cos(ang), (1, RET_HEADS)), jnp.tile(jnp.sin(ang), (1, RET_HEADS))


GDN_CG = 2
GDN_UNITS = GDN_CG * GDN_HEADS


def _gdn_kernel(cur_ref, halo_ref, gab_ref, gz_ref, cw_ref, alog_ref, dtb_ref, nw_ref, o_ref,
                state, a_s, x_s, xa_s, rhs_s, attn_s, qd_s, ks_s):
    c = CHUNK
    n = GDN_CG * c
    d = GDN_HEAD_DIM
    i = pl.program_id(0)

    @pl.when(i == 0)
    def _():
        state[...] = jnp.zeros_like(state)

    halo = jnp.where(i > 0, halo_ref[...], 0.0)
    xe = jnp.concatenate([halo, cur_ref[...]], axis=0)
    first = SUBLANES - (CONV_K - 1)
    y = cw_ref[0:1, :] * xe[first:first + n]
    for j in range(1, CONV_K):
        y = y + cw_ref[j:j + 1, :] * xe[first + j:first + j + n]
    qkv = _silu(y)

    gab = gab_ref[...]
    sp_in = gab + dtb_ref[...]
    softplus = jnp.maximum(sp_in, 0.0) + jnp.log1p(jnp.exp(-jnp.abs(sp_in)))
    g_all = -jnp.exp(alog_ref[...]) * softplus
    beta_all = jax.nn.sigmoid(gab)
    ri = lax.broadcasted_iota(jnp.int32, (c, c), 0)
    ci = lax.broadcasted_iota(jnp.int32, (c, c), 1)
    causal = ri >= ci
    strict = ri > ci
    eye = (ri == ci).astype(jnp.float32)
    ltri = causal.astype(jnp.float32)
    utri = (ri <= ci).astype(jnp.float32)
    keep = {}
    for cc in range(GDN_CG):
        rows = slice(cc * c, (cc + 1) * c)
        g_c = g_all[rows]
        gc_all = jnp.dot(ltri, g_c, preferred_element_type=jnp.float32)
        gc_rows = jnp.dot(g_c.T, utri, preferred_element_type=jnp.float32)
        for h in range(GDN_HEADS):
            un = cc * GDN_HEADS + h
            q = qkv[rows, h * d:(h + 1) * d]
            k = qkv[rows, GDN_WIDTH + h * d:GDN_WIDTH + (h + 1) * d]
            v = qkv[rows, 2 * GDN_WIDTH + h * d:2 * GDN_WIDTH + (h + 1) * d]
            q = q * lax.rsqrt(jnp.sum(q * q, axis=-1, keepdims=True) + NORM_EPS) * (d ** -0.5)
            k = k * lax.rsqrt(jnp.sum(k * k, axis=-1, keepdims=True) + NORM_EPS)
            gc = gc_all[:, h:h + 1]
            gr = gc_rows[h:h + 1, :]
            beta = beta_all[rows, GDN_HEADS + h:GDN_HEADS + h + 1]
            lmask = jnp.where(causal, jnp.exp(jnp.where(causal, gc - gr, 0.0)), 0.0)
            kb = k * beta
            a = jnp.where(strict, lax.dot_general(kb, k, (((1,), (1,)), ((), ())), preferred_element_type=jnp.float32) * lmask, 0.0)
            a_s[un] = a
            x_s[un] = eye - jnp.where(ri // 2 == ci // 2, a, 0.0)
            egc = jnp.exp(gc)
            rhs_s[un] = jnp.concatenate([v * beta, kb * egc], axis=1)
            attn_s[un] = lax.dot_general(q, k, (((1,), (1,)), ((), ())), preferred_element_type=jnp.float32) * lmask
            glast = gc_all[c - 1:c, h:h + 1]
            qd_s[un] = q * egc
            ks_s[un] = k * jnp.exp(glast - gc)
            keep[un] = jnp.exp(glast)
    b = 2
    while b < c:
        off = (ri // (2 * b) == ci // (2 * b)) & (ri % (2 * b) >= b) & (ci % (2 * b) < b)
        for un in range(GDN_UNITS):
            xa_s[un] = jnp.dot(x_s[un], jnp.where(off, a_s[un], 0.0), preferred_element_type=jnp.float32)
        for un in range(GDN_UNITS):
            x_s[un] = x_s[un] - jnp.dot(xa_s[un], x_s[un], preferred_element_type=jnp.float32)
        b *= 2
    for un in range(GDN_UNITS):
        rhs_s[un] = jnp.dot(x_s[un], rhs_s[un], preferred_element_type=jnp.float32)
    for cc in range(GDN_CG):
        rows = slice(cc * c, (cc + 1) * c)
        for h in range(GDN_HEADS):
            un = cc * GDN_HEADS + h
            sl = slice(h * d, (h + 1) * d)
            s = state[h]
            v_new = rhs_s[un, :, :d] - jnp.dot(rhs_s[un, :, d:], s, preferred_element_type=jnp.float32)
            o = jnp.dot(qd_s[un], s, preferred_element_type=jnp.float32) + jnp.dot(attn_s[un], v_new, preferred_element_type=jnp.float32)
            state[h] = s * keep[un] + lax.dot_general(ks_s[un], v_new, (((0,), (0,)), ((), ())),
                                                      preferred_element_type=jnp.float32)
            on = o * lax.rsqrt(jnp.mean(o * o, axis=-1, keepdims=True) + NORM_EPS) * nw_ref[...]
            o_ref[rows, sl] = on * _silu(gz_ref[rows, sl])


def _gdn(gqkv, gab, gz, conv_w, a_log, dt_bias, norm_w):
    t = gqkv.shape[0]
    n = GDN_CG * CHUNK
    d = GDN_HEAD_DIM
    lane_pad = lambda v: jnp.zeros((1, LANES), jnp.float32).at[0, :GDN_HEADS].set(v)
    tok = lambda w: pl.BlockSpec((n, w), lambda i: (i, 0))
    const = lambda shape: pl.BlockSpec(shape, lambda i: (0, 0))
    unit = pltpu.VMEM((GDN_UNITS, CHUNK, d), jnp.float32)
    return pl.pallas_call(
        _gdn_kernel,
        grid=(t // n,),
        in_specs=[tok(CONV_CH),
                  pl.BlockSpec((SUBLANES, CONV_CH), lambda i: (jnp.maximum(i * (n // SUBLANES) - 1, 0), 0)),
                  tok(LANES), tok(GDN_WIDTH), const((CONV_K, CONV_CH)), const((1, LANES)), const((1, LANES)),
                  const((1, d))],
        out_specs=tok(GDN_WIDTH),
        out_shape=jax.ShapeDtypeStruct((t, GDN_WIDTH), jnp.float32),
        scratch_shapes=[pltpu.VMEM((GDN_HEADS, d, d), jnp.float32), unit, unit, unit,
                        pltpu.VMEM((GDN_UNITS, CHUNK, 2 * d), jnp.float32), unit, unit, unit],
        compiler_params=pltpu.CompilerParams(dimension_semantics=("arbitrary",)),
        name="gdn",
    )(gqkv, gqkv, gab, gz, conv_w, lane_pad(a_log), lane_pad(dt_bias), norm_w.reshape(1, d))


MID_TM = 256


def _mid_kernel(x_ref, yr_ref, yg_ref, wo_ref, ln_ref, wq_ref, keys_ref, x1_ref, h2_ref, st_ref):
    mixed = jnp.concatenate([yr_ref[...], yg_ref[...]], axis=1).astype(jnp.bfloat16)
    x1 = x_ref[...] + jnp.dot(mixed, wo_ref[...], preferred_element_type=jnp.float32)
    x1_ref[...] = x1
    h2 = x1 * lax.rsqrt(jnp.mean(x1 * x1, axis=-1, keepdims=True) + NORM_EPS) * ln_ref[...]
    h2_ref[...] = h2
    q = jnp.dot(h2.astype(jnp.bfloat16), wq_ref[...], preferred_element_type=jnp.float32).astype(jnp.bfloat16)
    for hp in range(2 * PEER_HEADS):
        st_ref[hp] = lax.dot_general(keys_ref[hp], q[:, hp * PEER_HALF:(hp + 1) * PEER_HALF], (((1,), (1,)), ((), ())),
                                     preferred_element_type=jnp.float32)


def _mid_proj(x2d, y_r, y_g, w_out_b, ln2_w, wq_b, keys_b, tok0, t):
    assert tok0 % MID_TM == 0 and t % MID_TM == 0
    b0 = tok0 // MID_TM
    src = lambda w: pl.BlockSpec((MID_TM, w), lambda i: (b0 + i, 0))
    tok = lambda w: pl.BlockSpec((MID_TM, w), lambda i: (i, 0))
    nhp = 2 * PEER_HEADS
    return pl.pallas_call(
        _mid_kernel,
        grid=(t // MID_TM,),
        in_specs=[src(D_MODEL), src(RET_WIDTH), src(GDN_WIDTH),
                  pl.BlockSpec((D_MODEL, D_MODEL), lambda i: (0, 0)),
                  pl.BlockSpec((1, D_MODEL), lambda i: (0, 0)),
                  pl.BlockSpec(wq_b.shape, lambda i: (0, 0)),
                  pl.BlockSpec((nhp, PEER_KEYS, PEER_HALF), lambda i: (0, 0, 0))],
        out_specs=[tok(D_MODEL), tok(D_MODEL), pl.BlockSpec((nhp, PEER_KEYS, MID_TM), lambda i: (0, 0, i))],
        out_shape=[jax.ShapeDtypeStruct((t, D_MODEL), jnp.float32),
                   jax.ShapeDtypeStruct((t, D_MODEL), jnp.float32),
                   jax.ShapeDtypeStruct((nhp, PEER_KEYS, t), jnp.float32)],
        compiler_params=pltpu.CompilerParams(dimension_semantics=("arbitrary",), vmem_limit_bytes=VMEM_LIMIT),
        name="mid_proj",
    )(x2d, y_r, y_g, w_out_b, ln2_w.reshape(1, D_MODEL), wq_b, keys_b)


TOPK_TL = LANES
_PAIRS = [(a, b) for a in range(PEER_TOPK) for b in range(PEER_TOPK) if (a + 1) * (b + 1) <= PEER_TOPK]
_NPAIR_PAD = -(-len(_PAIRS) // SUBLANES) * SUBLANES
_NCAND = PEER_TOPK * PEER_TOPK


def _top16(s, n):
    iota = lax.broadcasted_iota(jnp.int32, s.shape, 0).astype(jnp.float32)
    vals, idxs = [], []
    for _ in range(PEER_TOPK):
        m = jnp.max(s, axis=0, keepdims=True)
        idx = jnp.min(jnp.where(s == m, iota, float(n)), axis=0, keepdims=True)
        vals.append(m)
        idxs.append(idx)
        s = jnp.where(iota == idx, -jnp.inf, s)
    return jnp.concatenate(vals, axis=0), jnp.concatenate(idxs, axis=0).astype(jnp.int32)


def _topk_kernel(st_ref, ids_ref, gate_ref):
    tl = st_ref.shape[2]
    npad = _NPAIR_PAD - len(_PAIRS)
    riota = lax.broadcasted_iota(jnp.int32, (_NPAIR_PAD, tl), 0)
    pos = jnp.full((_NPAIR_PAD, tl), _NCAND, jnp.int32)
    for r, (a, b) in enumerate(_PAIRS):
        pos = jnp.where(riota == r, a * PEER_TOPK + b, pos)
    ids_rows, gate_rows = [], []
    for h in range(PEER_HEADS):
        s1, i1 = _top16(st_ref[2 * h], PEER_KEYS)
        s2, i2 = _top16(st_ref[2 * h + 1], PEER_KEYS)
        cand = jnp.concatenate([s1[a:a + 1, :] + s2[b:b + 1, :] for a, b in _PAIRS]
                               + [jnp.full((npad, tl), -jnp.inf, jnp.float32)], axis=0)
        cid = jnp.concatenate([i1[a:a + 1, :] * PEER_KEYS + i2[b:b + 1, :] for a, b in _PAIRS]
                              + [jnp.zeros((npad, tl), jnp.int32)], axis=0)
        best, sel = [], []
        for _ in range(PEER_TOPK):
            m = jnp.max(cand, axis=0, keepdims=True)
            p = jnp.min(jnp.where(cand == m, pos, _NCAND + 1), axis=0, keepdims=True)
            hit = pos == p
            best.append(m)
            sel.append(jnp.max(jnp.where(hit, cid, -1), axis=0, keepdims=True))
            cand = jnp.where(hit, -jnp.inf, cand)
        best = jnp.concatenate(best, axis=0)
        e = jnp.exp(best - best[0:1, :])
        gate_rows.append(e / jnp.sum(e, axis=0, keepdims=True))
        ids_rows.append(jnp.concatenate(sel, axis=0))
    ids_ref[...] = jnp.concatenate(ids_rows, axis=0).T
    gate_ref[...] = jnp.concatenate(gate_rows, axis=0).T


def _peer_topk(st):
    t = st.shape[2]
    return pl.pallas_call(
        _topk_kernel,
        grid=(t // TOPK_TL,),
        in_specs=[pl.BlockSpec((2 * PEER_HEADS, PEER_KEYS, TOPK_TL), lambda i: (0, 0, i))],
        out_specs=[pl.BlockSpec((TOPK_TL, PEER_NK), lambda i: (i, 0)),
                   pl.BlockSpec((TOPK_TL, PEER_NK), lambda i: (i, 0))],
        out_shape=[jax.ShapeDtypeStruct((t, PEER_NK), jnp.int32),
                   jax.ShapeDtypeStruct((t, PEER_NK), jnp.float32)],
        compiler_params=pltpu.CompilerParams(dimension_semantics=("arbitrary",)),
        name="peer_topk",
    )(st)


PACK_TE = 256


def _pack_kernel(u_ref, v_ref, o_ref):
    hi = lax.bitcast_convert_type(u_ref[...].astype(jnp.bfloat16).astype(jnp.float32), jnp.uint32)
    lo = lax.bitcast_convert_type(v_ref[...].astype(jnp.bfloat16).astype(jnp.float32), jnp.uint32)
    w = hi | (lo >> 16)
    for c in range(D_MODEL // LANES):
        o_ref[:, c, 0, :] = w[:, c * LANES:(c + 1) * LANES]


def _peer_pack(u_tab, v_tab):
    ne = u_tab.shape[0]
    blk = pl.BlockSpec((PACK_TE, D_MODEL), lambda i: (i, 0))
    nch = D_MODEL // LANES
    return pl.pallas_call(
        _pack_kernel,
        grid=(ne // PACK_TE,),
        in_specs=[blk, blk],
        out_specs=pl.BlockSpec((PACK_TE, nch, 1, LANES), lambda i: (i, 0, 0, 0)),
        out_shape=jax.ShapeDtypeStruct((ne, nch, 1, LANES), jnp.uint32),
        compiler_params=pltpu.CompilerParams(dimension_semantics=("arbitrary",)),
        name="peer_pack",
    )(u_tab, v_tab)


PG_TB = 32
PG_SB = SUBLANES
PG_NSUB = PG_TB // PG_SB
PG_ROWS = PG_SB * PEER_NK
PG_NCH = D_MODEL // LANES


def _pg_issue(ids_ref, base, uv_ref, buf, sem, slot):
    def body(j, _):
        for k in range(PEER_NK):
            e = ids_ref[base + j * PEER_NK + k]
            pltpu.make_async_copy(uv_ref.at[e], buf.at[slot, :, pl.ds(j * PEER_NK + k, 1), :], sem.at[slot]).start()
        return 0
    lax.fori_loop(0, PG_SB, body, 0)


def _peer_gather_kernel(ids_ref, idsn_ref, gate_ref, h2_ref, x1_ref, lnf_ref, after_ref, uv_ref, o_ref, buf, sem):
    del after_ref
    i = pl.program_id(0)
    n = pl.num_programs(0)

    @pl.when(i == 0)
    def _():
        _pg_issue(ids_ref, 0, uv_ref, buf, sem, 0)

    sub = lax.broadcasted_iota(jnp.int32, (PG_SB, PEER_NK), 0)
    lane_tok = lax.broadcasted_iota(jnp.int32, (PG_SB, PG_ROWS), 1) // PEER_NK
    sub_w = lax.broadcasted_iota(jnp.int32, (PG_SB, PG_ROWS), 0)
    for s in range(PG_NSUB):
        slot = s % 2
        nslot = (s + 1) % 2
        if s + 1 < PG_NSUB:
            _pg_issue(ids_ref, (s + 1) * PG_ROWS, uv_ref, buf, sem, nslot)
        else:
            @pl.when(i + 1 < n)
            def _():
                _pg_issue(idsn_ref, 0, uv_ref, buf, sem, nslot)
        pltpu.make_async_copy(buf.at[slot], buf.at[slot], sem.at[slot]).wait()
        rows = pl.ds(s * PG_SB, PG_SB)
        h8 = h2_ref[rows, :]
        words = jnp.concatenate([buf[slot, c] for c in range(PG_NCH)], axis=1)
        u = lax.bitcast_convert_type(words & jnp.uint32(0xFFFF0000), jnp.float32)
        v = lax.bitcast_convert_type(words << 16, jnp.float32)
        dt = lax.dot_general(h8, u, (((1,), (1,)), ((), ())), preferred_element_type=jnp.float32)
        d = jnp.zeros((PG_SB, PEER_NK), jnp.float32)
        for j in range(PG_SB):
            d = d + jnp.where(sub == j, dt[:, j * PEER_NK:(j + 1) * PEER_NK], 0.0)
        act = 0.5 * d * (1.0 + lax.erf(d * (2.0 ** -0.5)))
        w = gate_ref[rows, :] * act
        wbd = jnp.where(lane_tok == sub_w, jnp.concatenate([w] * PG_SB, axis=1), 0.0)
        y = x1_ref[rows, :] + jnp.dot(wbd, v, preferred_element_type=jnp.float32)
        ms = jnp.mean(y * y, axis=-1, keepdims=True)
        o_ref[rows, :] = y * lax.rsqrt(ms + NORM_EPS) * lnf_ref[...]


def _peer_gather(ids_flat, gate, h2, x1, lnf_w, uv4, tok0, ntok, after):
    assert tok0 % PG_TB == 0 and ntok % PG_TB == 0
    nsteps = ntok // PG_TB
    b0 = tok0 // PG_TB
    blk = PG_TB * PEER_NK
    tok = lambda w: pl.BlockSpec((PG_TB, w), lambda i: (b0 + i, 0))
    return pl.pallas_call(
        _peer_gather_kernel,
        grid=(nsteps,),
        in_specs=[
            pl.BlockSpec((blk,), lambda i: (b0 + i,), memory_space=pltpu.SMEM),
            pl.BlockSpec((blk,), lambda i: (b0 + jnp.minimum(i + 1, nsteps - 1),), memory_space=pltpu.SMEM),
            tok(PEER_NK), tok(D_MODEL), tok(D_MODEL),
            pl.BlockSpec((1, D_MODEL), lambda i: (0, 0)),
            pl.BlockSpec((SUBLANES, LANES), lambda i: (0, 0)),
            pl.BlockSpec(memory_space=pl.ANY),
        ],
        out_specs=pl.BlockSpec((PG_TB, D_MODEL), lambda i: (i, 0)),
        out_shape=jax.ShapeDtypeStruct((ntok, D_MODEL), jnp.float32),
        scratch_shapes=[pltpu.VMEM((2, PG_NCH, PG_ROWS, LANES), jnp.uint32), pltpu.SemaphoreType.DMA((2,))],
        compiler_params=pltpu.CompilerParams(dimension_semantics=("arbitrary",), vmem_limit_bytes=VMEM_LIMIT),
        name="peer_gather",
    )(ids_flat, ids_flat, gate, h2, x1, lnf_w.reshape(1, D_MODEL), after, uv4)


SC_CORES = 2
SC_SUBCORES = 16
SC_LANES = 16
SC_WORKERS = SC_CORES * SC_SUBCORES
SC_G = 32
SC_NG = PEER_NK // SC_G
SC_NCH = D_MODEL // SC_LANES
SC_HC = SC_NCH // 2


def _sc_pipeline(tab_hbm, ids_hbm, vec_hbm, out_hbm, ids_v, vec_v, rows_v, obuf, gsem, isem, osem, tok_base, tpt, compute):
    wid = lax.axis_index("s") * SC_CORES + lax.axis_index("c")
    tok0 = wid * tpt

    def gather(ts, g):
        return pltpu.make_async_copy(tab_hbm.at[ids_v.at[ts, g]], rows_v.at[g % 2], gsem.at[g % 2])

    def load_inputs(j, ts):
        return (pltpu.make_async_copy(ids_hbm.at[tok_base + j], ids_v.at[ts], isem),
                pltpu.make_async_copy(vec_hbm.at[tok_base + j], vec_v.at[ts], isem))

    def out_copy(j, ts):
        return pltpu.make_async_copy(obuf.at[ts], out_hbm.at[j], osem.at[ts])

    for cp in load_inputs(tok0, 0):
        cp.start()
    for cp in load_inputs(tok0, 0):
        cp.wait()
    gather(0, 0).start()
    gather(0, 1).start()

    @pl.loop(0, tpt, step=2)
    def _(it):
        for ts in range(2):
            j = tok0 + it + ts
            more = (it + ts + 1) < tpt

            @pl.when(more)
            def _():
                for cp in load_inputs(j + 1, 1 - ts):
                    cp.start()

            @pl.when((it + ts) >= 2)
            def _():
                out_copy(j - 2, ts).wait()

            for g in range(SC_NG):
                gather(ts, g).wait()
                compute(ts, g)
                if g + 2 < SC_NG:
                    gather(ts, g + 2).start()
                else:
                    if g + 2 == SC_NG:
                        @pl.when(more)
                        def _():
                            for cp in load_inputs(j + 1, 1 - ts):
                                cp.wait()

                    @pl.when(more)
                    def _():
                        gather(1 - ts, g + 2 - SC_NG).start()
            out_copy(j, ts).start()

    out_copy(tok0 + tpt - 2, 0).wait()
    out_copy(tok0 + tpt - 1, 1).wait()


def _sc_call(body, out_width, vec_width, tok_base, ntok):
    assert ntok % (2 * SC_WORKERS) == 0
    return functools.partial(
        pl.kernel,
        mesh=plsc.VectorSubcoreMesh(core_axis_name="c", subcore_axis_name="s"),
        compiler_params=pltpu.CompilerParams(needs_layout_passes=False),
        out_type=jax.ShapeDtypeStruct((ntok, out_width), jnp.float32),
        scratch_types=[
            pltpu.VMEM((2, SC_NG, SC_G), jnp.int32),
            pltpu.VMEM((2, vec_width), jnp.float32),
            pltpu.VMEM((2, SC_G, D_MODEL), jnp.float32),
            pltpu.VMEM((2, out_width), jnp.float32),
            pltpu.SemaphoreType.DMA((2,)),
            pltpu.SemaphoreType.DMA,
            pltpu.SemaphoreType.DMA((2,)),
        ],
    )(functools.partial(body, tok_base=tok_base, tpt=ntok // SC_WORKERS))


def _sc_dots_body(u_hbm, ids_hbm, h_hbm, out_hbm, ids_v, h_v, rows_v, pbuf, gsem, isem, osem, *, tok_base, tpt):
    def compute(ts, g):
        def chunk(ci, accs):
            hc = h_v[ts, pl.ds(ci * SC_LANES, SC_LANES)]
            return tuple(accs[r] + rows_v[g % 2, r, pl.ds(ci * SC_LANES, SC_LANES)] * hc for r in range(SC_G))
        zeros = tuple(jnp.zeros((SC_LANES,), jnp.float32) for _ in range(SC_G))
        accs = plsc.parallel_loop(0, SC_NCH, carry=zeros)(chunk)
        for r in range(SC_G):
            pbuf[ts, pl.ds((g * SC_G + r) * SC_LANES, SC_LANES)] = accs[r]

    _sc_pipeline(u_hbm, ids_hbm, h_hbm, out_hbm, ids_v, h_v, rows_v, pbuf, gsem, isem, osem, tok_base, tpt, compute)


def _sc_wsum_body(v_hbm, ids_hbm, w_hbm, out_hbm, ids_v, w_v, rows_v, obuf, gsem, isem, osem, *, tok_base, tpt):
    def compute(ts, g):
        for half in range(2):
            def row(r, accs):
                wr = plsc.load_gather(w_v.at[ts], [jnp.full((SC_LANES,), g * SC_G, jnp.int32) + r])
                return tuple(accs[c] + rows_v[g % 2, r, pl.ds((half * SC_HC + c) * SC_LANES, SC_LANES)] * wr
                             for c in range(SC_HC))
            zeros = tuple(jnp.zeros((SC_LANES,), jnp.float32) for _ in range(SC_HC))
            accs = plsc.parallel_loop(0, SC_G, carry=zeros)(row)
            for c in range(SC_HC):
                sl = pl.ds((half * SC_HC + c) * SC_LANES, SC_LANES)
                if g == 0:
                    obuf[ts, sl] = accs[c]
                else:
                    plsc.addupdate(obuf.at[ts, sl], accs[c])

    _sc_pipeline(v_hbm, ids_hbm, w_hbm, out_hbm, ids_v, w_v, rows_v, obuf, gsem, isem, osem, tok_base, tpt, compute)


SCT_TM = 256


def _peer_act_kernel(dp_ref, gate_ref, after_ref, w_ref):
    del after_ref
    lane_exp = lax.broadcasted_iota(jnp.int32, (PEER_NK * SC_LANES, PEER_NK), 0) // SC_LANES
    fold = (lane_exp == lax.broadcasted_iota(jnp.int32, (PEER_NK * SC_LANES, PEER_NK), 1)).astype(jnp.float32)
    d = jnp.dot(dp_ref[...], fold, preferred_element_type=jnp.float32)
    w_ref[...] = gate_ref[...] * (0.5 * d * (1.0 + lax.erf(d * (2.0 ** -0.5))))


def _peer_act(dpart, gate, tok0, after):
    n = dpart.shape[0]
    b0 = tok0 // SCT_TM
    return pl.pallas_call(
        _peer_act_kernel,
        grid=(n // SCT_TM,),
        in_specs=[pl.BlockSpec((SCT_TM, PEER_NK * SC_LANES), lambda i: (i, 0)),
                  pl.BlockSpec((SCT_TM, PEER_NK), lambda i: (b0 + i, 0)),
                  pl.BlockSpec((SUBLANES, LANES), lambda i: (0, 0))],
        out_specs=pl.BlockSpec((SCT_TM, PEER_NK), lambda i: (i, 0)),
        out_shape=jax.ShapeDtypeStruct((n, PEER_NK), jnp.float32),
        compiler_params=pltpu.CompilerParams(dimension_semantics=("arbitrary",)),
        name="peer_act",
    )(dpart, gate, after)


def _peer_final_kernel(x1_ref, p_ref, lnf_ref, o_ref):
    y = x1_ref[...] + p_ref[...]
    o_ref[...] = y * lax.rsqrt(jnp.mean(y * y, axis=-1, keepdims=True) + NORM_EPS) * lnf_ref[...]


def _peer_final(x1, peer, lnf_w, tok0, peer0):
    n = x1.shape[0] - tok0
    assert tok0 % SCT_TM == 0 and peer0 % SCT_TM == 0 and n % SCT_TM == 0
    b0, p0 = tok0 // SCT_TM, peer0 // SCT_TM
    return pl.pallas_call(
        _peer_final_kernel,
        grid=(n // SCT_TM,),
        in_specs=[pl.BlockSpec((SCT_TM, D_MODEL), lambda i: (b0 + i, 0)),
                  pl.BlockSpec((SCT_TM, D_MODEL), lambda i: (p0 + i, 0)),
                  pl.BlockSpec((1, D_MODEL), lambda i: (0, 0))],
        out_specs=pl.BlockSpec((SCT_TM, D_MODEL), lambda i: (i, 0)),
        out_shape=jax.ShapeDtypeStruct((n, D_MODEL), jnp.float32),
        compiler_params=pltpu.CompilerParams(dimension_semantics=("arbitrary",)),
        name="peer_final",
    )(x1, peer, lnf_w.reshape(1, D_MODEL))


PEER_TC_A = 2592
PEER_TC_B = 4320
PEER_SC_FIRST = 4608


def _peer_mixer(t, route, lnf_w, u_tab, v_tab):
    n_tc = PEER_TC_A + PEER_TC_B
    n_sc = t - n_tc
    sc_parts = (PEER_SC_FIRST, n_sc - PEER_SC_FIRST)
    routed, dots, tok0 = [], [], n_tc
    for n in sc_parts:
        x1_s, h2_s, ids_s, gate_s = route(tok0, n)
        ids_g = ids_s.reshape(n, SC_NG, SC_G)
        dots.append(_sc_call(_sc_dots_body, PEER_NK * SC_LANES, D_MODEL, 0, n)(u_tab, ids_g, h2_s))
        routed.append((x1_s, ids_g, gate_s))
        tok0 += n
    uv4 = _peer_pack(u_tab, v_tab)
    x1_t, h2_t, ids_t, gate_t = route(0, n_tc)
    ids_flat = ids_t.reshape(n_tc * PEER_NK)
    out_a = _peer_gather(ids_flat, gate_t, h2_t, x1_t, lnf_w, uv4, 0, PEER_TC_A, gate_t[:SUBLANES])
    w_sc = jnp.concatenate([_peer_act(d, r[2], 0, out_a[:SUBLANES, :LANES]) for d, r in zip(dots, routed)], axis=0)
    ids_sc = jnp.concatenate([r[1] for r in routed], axis=0)
    peer_sc = _sc_call(_sc_wsum_body, D_MODEL, PEER_NK, 0, n_sc)(v_tab, ids_sc, w_sc)
    out_b = _peer_gather(ids_flat, gate_t, h2_t, x1_t, lnf_w, uv4, PEER_TC_A, PEER_TC_B, w_sc[:SUBLANES])
    outs, off = [out_a, out_b], 0
    for n, r in zip(sc_parts, routed):
        outs.append(_peer_final(r[0], peer_sc, lnf_w, 0, off))
        off += n
    return jnp.concatenate(outs, axis=0)


def kernel(x, ln1_w, w_in, ret_norm_w, conv_w, A_log, dt_bias, gdn_norm_w, w_out, ln2_w, peer_wq, peer_keys, peer_u, peer_v, lnf_w):
    b, t, d = x.shape
    assert b == 1 and d == D_MODEL and t % MID_TM == 0 and ln1_w.shape[0] == 1
    l = 0
    x2d = x.reshape(t, d)

    cols = _in_proj_cols()
    w_p = jnp.where((cols >= 0)[None, :], jnp.take(w_in[l], jnp.maximum(cols, 0), axis=1), 0.0).astype(jnp.bfloat16)
    keys_b = peer_keys[l].reshape(2 * PEER_HEADS, PEER_KEYS, PEER_HALF).astype(jnp.bfloat16)
    w_out_b, wq_b = w_out[l].astype(jnp.bfloat16), peer_wq[l].astype(jnp.bfloat16)
    cos4, sin4 = _rope_tables(t)

    rqk, rv, rg, gqkv, gz, gab = _in_proj(x2d, ln1_w[l], w_p)
    y_r = _retention(rqk, rv, rg, cos4, sin4, ret_norm_w[l])
    y_g = _gdn(gqkv, gab, gz, conv_w[l], A_log[l], dt_bias[l], gdn_norm_w[l])

    def route(tok0, n):
        x1, h2, st = _mid_proj(x2d, y_r, y_g, w_out_b, ln2_w[l], wq_b, keys_b, tok0, n)
        ids, gate = _peer_topk(st)
        return x1, h2, ids, gate

    out = _peer_mixer(t, route, lnf_w, peer_u[l], peer_v[l])
    return out.reshape(b, t, d)
```

```python
import math

import jax
import jax.numpy as jnp
import numpy as np
from jax import lax
import functools

from jax.experimental import pallas as pl
from jax.experimental.pallas import tpu as pltpu
from jax.experimental.pallas import tpu_sc as plsc

D_MODEL = 1024
RET_HEADS = 4
RET_QK_DIM = 64
RET_HALF = RET_QK_DIM // 2
RET_V_DIM = 128
RET_WIDTH = RET_HEADS * RET_V_DIM
GDN_HEADS = 4
GDN_HEAD_DIM = 128
GDN_WIDTH = GDN_HEADS * GDN_HEAD_DIM
CONV_K = 4
CONV_CH = 3 * GDN_WIDTH
CHUNK = 128
ROPE_BASE = 10000.0
NORM_EPS = 1e-6
PEER_HEADS = 8
PEER_KEYS = 128
PEER_HALF = 128
PEER_TOPK = 16
PEER_NK = PEER_HEADS * PEER_TOPK
LANES = 128
SUBLANES = 8
VMEM_LIMIT = 48 * 1024 * 1024


def _silu(x):
    return x * jax.nn.sigmoid(x)


IN_TM = 256
IN_WIDTHS = (2 * RET_HEADS * RET_QK_DIM, RET_WIDTH, RET_WIDTH, CONV_CH, GDN_WIDTH, LANES)


def _in_proj_cols():
    splits = (256, 256, 512, 512, 512, 512, 512, 512, 4, 4)
    rq, rk, rv, rg, gq, _, _, gz, ga, _ = np.cumsum((0,) + splits)[:10]

    def rotary_halves(base):
        first = [base + h * RET_QK_DIM + d for h in range(RET_HEADS) for d in range(RET_HALF)]
        second = [base + h * RET_QK_DIM + RET_HALF + d for h in range(RET_HEADS) for d in range(RET_HALF)]
        return first + second

    cols = rotary_halves(rq) + rotary_halves(rk)
    cols += list(range(rv, rv + RET_WIDTH)) + list(range(rg, rg + RET_WIDTH))
    cols += list(range(gq, gq + CONV_CH)) + list(range(gz, gz + GDN_WIDTH))
    cols += list(range(ga, ga + 2 * GDN_HEADS)) + [-1] * (LANES - 2 * GDN_HEADS)
    return np.asarray(cols, np.int32)


def _in_proj_kernel(x_ref, ln_ref, w_ref, *out_refs):
    x = x_ref[...]
    h = x * lax.rsqrt(jnp.mean(x * x, axis=-1, keepdims=True) + NORM_EPS) * ln_ref[...]
    p = jnp.dot(h.astype(jnp.bfloat16), w_ref[...], preferred_element_type=jnp.float32)
    off = 0
    for ref, width in zip(out_refs, IN_WIDTHS):
        ref[...] = p[:, off:off + width]
        off += width


def _in_proj(x2d, ln_w, w_p, tok0, t):
    assert tok0 % IN_TM == 0 and t % IN_TM == 0
    b0 = tok0 // IN_TM
    return pl.pallas_call(
        _in_proj_kernel,
        grid=(t // IN_TM,),
        in_specs=[pl.BlockSpec((IN_TM, D_MODEL), lambda i: (b0 + i, 0)),
                  pl.BlockSpec((1, D_MODEL), lambda i: (0, 0)),
                  pl.BlockSpec(w_p.shape, lambda i: (0, 0))],
        out_specs=[pl.BlockSpec((IN_TM, w), lambda i: (i, 0)) for w in IN_WIDTHS],
        out_shape=[jax.ShapeDtypeStruct((t, w), jnp.float32) for w in IN_WIDTHS],
        compiler_params=pltpu.CompilerParams(dimension_semantics=("arbitrary",), vmem_limit_bytes=VMEM_LIMIT),
        name="in_proj",
    )(x2d, ln_w.reshape(1, D_MODEL), w_p)


def _retention_kernel(rqk_ref, rv_ref, rg_ref, cos_ref, sin_ref, nw_ref, s_in_ref, o_ref, s_out_ref, state):
    c = CHUNK

    @pl.when(pl.program_id(0) == 0)
    def _():
        state[...] = s_in_ref[...]

    cos, sin = cos_ref[...], sin_ref[...]
    qa, qb = rqk_ref[:, 0:128], rqk_ref[:, 128:256]
    ka, kb = rqk_ref[:, 256:384], rqk_ref[:, 384:512]
    q = jnp.concatenate([qa * cos - qb * sin, qa * sin + qb * cos], axis=1)
    k = jnp.concatenate([ka * cos - kb * sin, ka * sin + kb * cos], axis=1) * (RET_QK_DIM ** -0.5)
    lane_head = (lax.broadcasted_iota(jnp.int32, (1, 2 * LANES), 1) % LANES) // RET_HALF
    ri = lax.broadcasted_iota(jnp.int32, (c, c), 0)
    ci = lax.broadcasted_iota(jnp.int32, (c, c), 1)
    diff = (ri - ci).astype(jnp.float32)
    pos = lax.broadcasted_iota(jnp.int32, (c, 1), 0).astype(jnp.float32)
    for h in range(RET_HEADS):
        lg = math.log1p(-(2.0 ** (-5.0 - h)))
        head_lanes = lane_head == h
        qh = jnp.where(head_lanes, q, 0.0)
        kh = jnp.where(head_lanes, k, 0.0)
        sl = slice(h * RET_V_DIM, (h + 1) * RET_V_DIM)
        v = rv_ref[:, sl]
        decay = jnp.where(diff >= 0, jnp.exp(lg * jnp.maximum(diff, 0.0)), 0.0)
        scores = lax.dot_general(qh, k, (((1,), (1,)), ((), ())), preferred_element_type=jnp.float32) * decay
        intra = jnp.dot(scores, v, preferred_element_type=jnp.float32)
        s_prev = state[h]
        inter = jnp.dot(qh * jnp.exp(lg * (pos + 1.0)), s_prev, preferred_element_type=jnp.float32)
        o = intra + inter
        k_dec = kh * jnp.exp(lg * (c - 1.0 - pos))
        state[h] = s_prev * math.exp(lg * c) + lax.dot_general(k_dec, v, (((0,), (0,)), ((), ())),
                                                              preferred_element_type=jnp.float32)
        mu = jnp.mean(o, axis=-1, keepdims=True)
        var = jnp.mean(jnp.square(o - mu), axis=-1, keepdims=True)
        o_ref[:, sl] = (o - mu) * lax.rsqrt(var + NORM_EPS) * nw_ref[:, sl] * _silu(rg_ref[:, sl])

    @pl.when(pl.program_id(0) == pl.num_programs(0) - 1)
    def _():
        s_out_ref[...] = state[...]


def _retention(rqk, rv, rg, cos4, sin4, norm_w, state_in, pos0):
    t = rqk.shape[0]
    c = CHUNK
    assert pos0 % c == 0 and t % c == 0
    p0 = pos0 // c
    tok = lambda w: pl.BlockSpec((c, w), lambda i: (i, 0))
    pos = pl.BlockSpec((c, LANES), lambda i: (p0 + i, 0))
    st_shape = (RET_HEADS, 2 * LANES, RET_V_DIM)
    st = pl.BlockSpec(st_shape, lambda i: (0, 0, 0))
    return pl.pallas_call(
        _retention_kernel,
        grid=(t // c,),
        in_specs=[tok(512), tok(RET_WIDTH), tok(RET_WIDTH), pos, pos,
                  pl.BlockSpec((1, RET_WIDTH), lambda i: (0, 0)), st],
        out_specs=[tok(RET_WIDTH), st],
        out_shape=[jax.ShapeDtypeStruct((t, RET_WIDTH), jnp.float32), jax.ShapeDtypeStruct(st_shape, jnp.float32)],
        scratch_shapes=[pltpu.VMEM(st_shape, jnp.float32)],
        compiler_params=pltpu.CompilerParams(dimension_semantics=("arbitrary",)),
        name="retention",
    )(rqk, rv, rg, cos4, sin4, norm_w.reshape(1, RET_WIDTH), state_in)


def _rope_tables(t):
    inv = ROPE_BASE ** (-jnp.arange(RET_HALF, dtype=jnp.float32) / RET_HALF)
    ang = jnp.arange(t, dtype=jnp.float32)[:, None] * inv[None, :]
    return jnp.tile(jnp.cos(ang), (1, RET_HEADS)), jnp.tile(jnp.sin(ang), (1, RET_HEADS))


GDN_CG = 2
GDN_UNITS = GDN_CG * GDN_HEADS


def _gdn_kernel(cur_ref, halo_ref, tail_ref, gab_ref, gz_ref, cw_ref, alog_ref, dtb_ref, nw_ref, s_in_ref,
                o_ref, s_out_ref, state, a_s, x_s, xa_s, rhs_s, attn_s, qd_s, ks_s):
    c = CHUNK
    n = GDN_CG * c
    d = GDN_HEAD_DIM
    i = pl.program_id(0)

    @pl.when(i == 0)
    def _():
        state[...] = s_in_ref[...]

    halo = jnp.where(i > 0, halo_ref[...], tail_ref[...])
    xe = jnp.concatenate([halo, cur_ref[...]], axis=0)
    first = SUBLANES - (CONV_K - 1)
    y = cw_ref[0:1, :] * xe[first:first + n]
    for j in range(1, CONV_K):
        y = y + cw_ref[j:j + 1, :] * xe[first + j:first + j + n]
    qkv = _silu(y)

    gab = gab_ref[...]
    sp_in = gab + dtb_ref[...]
    softplus = jnp.maximum(sp_in, 0.0) + jnp.log1p(jnp.exp(-jnp.abs(sp_in)))
    g_all = -jnp.exp(alog_ref[...]) * softplus
    beta_all = jax.nn.sigmoid(gab)
    ri = lax.broadcasted_iota(jnp.int32, (c, c), 0)
    ci = lax.broadcasted_iota(jnp.int32, (c, c), 1)
    causal = ri >= ci
    strict = ri > ci
    eye = (ri == ci).astype(jnp.float32)
    ltri = causal.astype(jnp.float32)
    utri = (ri <= ci).astype(jnp.float32)
    keep = {}
    for cc in range(GDN_CG):
        rows = slice(cc * c, (cc + 1) * c)
        g_c = g_all[rows]
        gc_all = jnp.dot(ltri, g_c, preferred_element_type=jnp.float32)
        gc_rows = jnp.dot(g_c.T, utri, preferred_element_type=jnp.float32)
        for h in range(GDN_HEADS):
            un = cc * GDN_HEADS + h
            q = qkv[rows, h * d:(h + 1) * d]
            k = qkv[rows, GDN_WIDTH + h * d:GDN_WIDTH + (h + 1) * d]
            v = qkv[rows, 2 * GDN_WIDTH + h * d:2 * GDN_WIDTH + (h + 1) * d]
            q = q * lax.rsqrt(jnp.sum(q * q, axis=-1, keepdims=True) + NORM_EPS) * (d ** -0.5)
            k = k * lax.rsqrt(jnp.sum(k * k, axis=-1, keepdims=True) + NORM_EPS)
            gc = gc_all[:, h:h + 1]
            gr = gc_rows[h:h + 1, :]
            beta = beta_all[rows, GDN_HEADS + h:GDN_HEADS + h + 1]
            lmask = jnp.where(causal, jnp.exp(jnp.where(causal, gc - gr, 0.0)), 0.0)
            kb = k * beta
            a = jnp.where(strict, lax.dot_general(kb, k, (((1,), (1,)), ((), ())), preferred_element_type=jnp.float32) * lmask, 0.0)
            a_s[un] = a
            x_s[un] = eye - jnp.where(ri // 2 == ci // 2, a, 0.0)
            egc = jnp.exp(gc)
            rhs_s[un] = jnp.concatenate([v * beta, kb * egc], axis=1)
            attn_s[un] = lax.dot_general(q, k, (((1,), (1,)), ((), ())), preferred_element_type=jnp.float32) * lmask
            glast = gc_all[c - 1:c, h:h + 1]
            qd_s[un] = q * egc
            ks_s[un] = k * jnp.exp(glast - gc)
            keep[un] = jnp.exp(glast)
    b = 2
    while b < c:
        off = (ri // (2 * b) == ci // (2 * b)) & (ri % (2 * b) >= b) & (ci % (2 * b) < b)
        for un in range(GDN_UNITS):
            xa_s[un] = jnp.dot(x_s[un], jnp.where(off, a_s[un], 0.0), preferred_element_type=jnp.float32)
        for un in range(GDN_UNITS):
            x_s[un] = x_s[un] - jnp.dot(xa_s[un], x_s[un], preferred_element_type=jnp.float32)
        b *= 2
    for un in range(GDN_UNITS):
        rhs_s[un] = jnp.dot(x_s[un], rhs_s[un], preferred_element_type=jnp.float32)
    for cc in range(GDN_CG):
        rows = slice(cc * c, (cc + 1) * c)
        for h in range(GDN_HEADS):
            un = cc * GDN_HEADS + h
            sl = slice(h * d, (h + 1) * d)
            s = state[h]
            v_new = rhs_s[un, :, :d] - jnp.dot(rhs_s[un, :, d:], s, preferred_element_type=jnp.float32)
            o = jnp.dot(qd_s[un], s, preferred_element_type=jnp.float32) + jnp.dot(attn_s[un], v_new, preferred_element_type=jnp.float32)
            state[h] = s * keep[un] + lax.dot_general(ks_s[un], v_new, (((0,), (0,)), ((), ())),
                                                      preferred_element_type=jnp.float32)
            on = o * lax.rsqrt(jnp.mean(o * o, axis=-1, keepdims=True) + NORM_EPS) * nw_ref[...]
            o_ref[rows, sl] = on * _silu(gz_ref[rows, sl])

    @pl.when(i == pl.num_programs(0) - 1)
    def _():
        s_out_ref[...] = state[...]


def _gdn(gqkv, gab, gz, conv_w, a_log, dt_bias, norm_w, state_in, prev_tail):
    t = gqkv.shape[0]
    n = GDN_CG * CHUNK
    d = GDN_HEAD_DIM
    assert t % n == 0
    lane_pad = lambda v: jnp.zeros((1, LANES), jnp.float32).at[0, :GDN_HEADS].set(v)
    tok = lambda w: pl.BlockSpec((n, w), lambda i: (i, 0))
    const = lambda shape: pl.BlockSpec(shape, lambda i: (0, 0))
    unit = pltpu.VMEM((GDN_UNITS, CHUNK, d), jnp.float32)
    st_shape = (GDN_HEADS, d, d)
    st = pl.BlockSpec(st_shape, lambda i: (0, 0, 0))
    return pl.pallas_call(
        _gdn_kernel,
        grid=(t // n,),
        in_specs=[tok(CONV_CH),
                  pl.BlockSpec((SUBLANES, CONV_CH), lambda i: (jnp.maximum(i * (n // SUBLANES) - 1, 0), 0)),
                  const((SUBLANES, CONV_CH)),
                  tok(LANES), tok(GDN_WIDTH), const((CONV_K, CONV_CH)), const((1, LANES)), const((1, LANES)),
                  const((1, d)), st],
        out_specs=[tok(GDN_WIDTH), st],
        out_shape=[jax.ShapeDtypeStruct((t, GDN_WIDTH), jnp.float32), jax.ShapeDtypeStruct(st_shape, jnp.float32)],
        scratch_shapes=[pltpu.VMEM(st_shape, jnp.float32), unit, unit, unit,
                        pltpu.VMEM((GDN_UNITS, CHUNK, 2 * d), jnp.float32), unit, unit, unit],
        compiler_params=pltpu.CompilerParams(dimension_semantics=("arbitrary",)),
        name="gdn",
    )(gqkv, gqkv, prev_tail, gab, gz, conv_w, lane_pad(a_log), lane_pad(dt_bias), norm_w.reshape(1, d), state_in)


MID_TM = 256


def _mid_kernel(x_ref, yr_ref, yg_ref, wo_ref, ln_ref, wq_ref, keys_ref, x1_ref, h2_ref, st_ref):
    mixed = jnp.concatenate([yr_ref[...], yg_ref[...]], axis=1).astype(jnp.bfloat16)
    x1 = x_ref[...] + jnp.dot(mixed, wo_ref[...], preferred_element_type=jnp.float32)
    x1_ref[...] = x1
    h2 = x1 * lax.rsqrt(jnp.mean(x1 * x1, axis=-1, keepdims=True) + NORM_EPS) * ln_ref[...]
    h2_ref[...] = h2
    q = jnp.dot(h2.astype(jnp.bfloat16), wq_ref[...], preferred_element_type=jnp.float32).astype(jnp.bfloat16)
    for hp in range(2 * PEER_HEADS):
        st_ref[hp] = lax.dot_general(keys_ref[hp], q[:, hp * PEER_HALF:(hp + 1) * PEER_HALF], (((1,), (1,)), ((), ())),
                                     preferred_element_type=jnp.float32)


def _mid_proj(x2d, y_r, y_g, w_out_b, ln2_w, wq_b, keys_b, x0, y0, t):
    assert x0 % MID_TM == 0 and y0 % MID_TM == 0 and t % MID_TM == 0
    bx, by = x0 // MID_TM, y0 // MID_TM
    tok = lambda w: pl.BlockSpec((MID_TM, w), lambda i: (i, 0))
    nhp = 2 * PEER_HEADS
    return pl.pallas_call(
        _mid_kernel,
        grid=(t // MID_TM,),
        in_specs=[pl.BlockSpec((MID_TM, D_MODEL), lambda i: (bx + i, 0)),
                  pl.BlockSpec((MID_TM, RET_WIDTH), lambda i: (by + i, 0)),
                  pl.BlockSpec((MID_TM, GDN_WIDTH), lambda i: (by + i, 0)),
                  pl.BlockSpec((D_MODEL, D_MODEL), lambda i: (0, 0)),
                  pl.BlockSpec((1, D_MODEL), lambda i: (0, 0)),
                  pl.BlockSpec(wq_b.shape, lambda i: (0, 0)),
                  pl.BlockSpec((nhp, PEER_KEYS, PEER_HALF), lambda i: (0, 0, 0))],
        out_specs=[tok(D_MODEL), tok(D_MODEL), pl.BlockSpec((nhp, PEER_KEYS, MID_TM), lambda i: (0, 0, i))],
        out_shape=[jax.ShapeDtypeStruct((t, D_MODEL), jnp.float32),
                   jax.ShapeDtypeStruct((t, D_MODEL), jnp.float32),
                   jax.ShapeDtypeStruct((nhp, PEER_KEYS, t), jnp.float32)],
        compiler_params=pltpu.CompilerParams(dimension_semantics=("arbitrary",), vmem_limit_bytes=VMEM_LIMIT),
        name="mid_proj",
    )(x2d, y_r, y_g, w_out_b, ln2_w.reshape(1, D_MODEL), wq_b, keys_b)


TOPK_TL = LANES
_PAIRS = [(a, b) for a in range(PEER_TOPK) for b in range(PEER_TOPK) if (a + 1) * (b + 1) <= PEER_TOPK]
_NPAIR_PAD = -(-len(_PAIRS) // SUBLANES) * SUBLANES
_NCAND = PEER_TOPK * PEER_TOPK


def _top16(s, n):
    iota = lax.broadcasted_iota(jnp.int32, s.shape, 0).astype(jnp.float32)
    vals, idxs = [], []
    for _ in range(PEER_TOPK):
        m = jnp.max(s, axis=0, keepdims=True)
        idx = jnp.min(jnp.where(s == m, iota, float(n)), axis=0, keepdims=True)
        vals.append(m)
        idxs.append(idx)
        s = jnp.where(iota == idx, -jnp.inf, s)
    return jnp.concatenate(vals, axis=0), jnp.concatenate(idxs, axis=0).astype(jnp.int32)


def _topk_kernel(st_ref, ids_ref, gate_ref):
    tl = st_ref.shape[2]
    npad = _NPAIR_PAD - len(_PAIRS)
    riota = lax.broadcasted_iota(jnp.int32, (_NPAIR_PAD, tl), 0)
    pos = jnp.full((_NPAIR_PAD, tl), _NCAND, jnp.int32)
    for r, (a, b) in enumerate(_PAIRS):
        pos = jnp.where(riota == r, a * PEER_TOPK + b, pos)
    ids_rows, gate_rows = [], []
    for h in range(PEER_HEADS):
        s1, i1 = _top16(st_ref[2 * h], PEER_KEYS)
        s2, i2 = _top16(st_ref[2 * h + 1], PEER_KEYS)
        cand = jnp.concatenate([s1[a:a + 1, :] + s2[b:b + 1, :] for a, b in _PAIRS]
                               + [jnp.full((npad, tl), -jnp.inf, jnp.float32)], axis=0)
        cid = jnp.concatenate([i1[a:a + 1, :] * PEER_KEYS + i2[b:b + 1, :] for a, b in _PAIRS]
                              + [jnp.zeros((npad, tl), jnp.int32)], axis=0)
        best, sel = [], []
        for _ in range(PEER_TOPK):
            m = jnp.max(cand, axis=0, keepdims=True)
            p = jnp.min(jnp.where(cand == m, pos, _NCAND + 1), axis=0, keepdims=True)
            hit = pos == p
            best.append(m)
            sel.append(jnp.max(jnp.where(hit, cid, -1), axis=0, keepdims=True))
            cand = jnp.where(hit, -jnp.inf, cand)
        best = jnp.concatenate(best, axis=0)
        e = jnp.exp(best - best[0:1, :])
        gate_rows.append(e / jnp.sum(e, axis=0, keepdims=True))
        ids_rows.append(jnp.concatenate(sel, axis=0))
    ids_ref[...] = jnp.concatenate(ids_rows, axis=0).T
    gate_ref[...] = jnp.concatenate(gate_rows, axis=0).T


def _peer_topk(st):
    t = st.shape[2]
    return pl.pallas_call(
        _topk_kernel,
        grid=(t // TOPK_TL,),
        in_specs=[pl.BlockSpec((2 * PEER_HEADS, PEER_KEYS, TOPK_TL), lambda i: (0, 0, i))],
        out_specs=[pl.BlockSpec((TOPK_TL, PEER_NK), lambda i: (i, 0)),
                   pl.BlockSpec((TOPK_TL, PEER_NK), lambda i: (i, 0))],
        out_shape=[jax.ShapeDtypeStruct((t, PEER_NK), jnp.int32),
                   jax.ShapeDtypeStruct((t, PEER_NK), jnp.float32)],
        compiler_params=pltpu.CompilerParams(dimension_semantics=("arbitrary",)),
        name="peer_topk",
    )(st)


PACK_TE = 256


def _pack_kernel(u_ref, v_ref, o_ref):
    hi = lax.bitcast_convert_type(u_ref[...].astype(jnp.bfloat16).astype(jnp.float32), jnp.uint32)
    lo = lax.bitcast_convert_type(v_ref[...].astype(jnp.bfloat16).astype(jnp.float32), jnp.uint32)
    w = hi | (lo >> 16)
    for c in range(D_MODEL // LANES):
        o_ref[:, c, 0, :] = w[:, c * LANES:(c + 1) * LANES]


def _peer_pack(u_tab, v_tab):
    ne = u_tab.shape[0]
    blk = pl.BlockSpec((PACK_TE, D_MODEL), lambda i: (i, 0))
    nch = D_MODEL // LANES
    return pl.pallas_call(
        _pack_kernel,
        grid=(ne // PACK_TE,),
        in_specs=[blk, blk],
        out_specs=pl.BlockSpec((PACK_TE, nch, 1, LANES), lambda i: (i, 0, 0, 0)),
        out_shape=jax.ShapeDtypeStruct((ne, nch, 1, LANES), jnp.uint32),
        compiler_params=pltpu.CompilerParams(dimension_semantics=("arbitrary",)),
        name="peer_pack",
    )(u_tab, v_tab)


PG_TB = 32
PG_SB = SUBLANES
PG_NSUB = PG_TB // PG_SB
PG_ROWS = PG_SB * PEER_NK
PG_NCH = D_MODEL // LANES


def _pg_issue(ids_ref, base, uv_ref, buf, sem, slot):
    def body(j, _):
        for k in range(PEER_NK):
            e = ids_ref[base + j * PEER_NK + k]
            pltpu.make_async_copy(uv_ref.at[e], buf.at[slot, :, pl.ds(j * PEER_NK + k, 1), :], sem.at[slot]).start()
        return 0
    lax.fori_loop(0, PG_SB, body, 0)


def _peer_gather_kernel(ids_ref, idsn_ref, gate_ref, h2_ref, x1_ref, lnf_ref, after_ref, uv_ref, o_ref, buf, sem):
    del after_ref
    i = pl.program_id(0)
    n = pl.num_programs(0)

    @pl.when(i == 0)
    def _():
        _pg_issue(ids_ref, 0, uv_ref, buf, sem, 0)

    sub = lax.broadcasted_iota(jnp.int32, (PG_SB, PEER_NK), 0)
    lane_tok = lax.broadcasted_iota(jnp.int32, (PG_SB, PG_ROWS), 1) // PEER_NK
    sub_w = lax.broadcasted_iota(jnp.int32, (PG_SB, PG_ROWS), 0)
    for s in range(PG_NSUB):
        slot = s % 2
        nslot = (s + 1) % 2
        if s + 1 < PG_NSUB:
            _pg_issue(ids_ref, (s + 1) * PG_ROWS, uv_ref, buf, sem, nslot)
        else:
            @pl.when(i + 1 < n)
            def _():
                _pg_issue(idsn_ref, 0, uv_ref, buf, sem, nslot)
        pltpu.make_async_copy(buf.at[slot], buf.at[slot], sem.at[slot]).wait()
        rows = pl.ds(s * PG_SB, PG_SB)
        h8 = h2_ref[rows, :]
        words = jnp.concatenate([buf[slot, c] for c in range(PG_NCH)], axis=1)
        u = lax.bitcast_convert_type(words & jnp.uint32(0xFFFF0000), jnp.float32)
        v = lax.bitcast_convert_type(words << 16, jnp.float32)
        dt = lax.dot_general(h8, u, (((1,), (1,)), ((), ())), preferred_element_type=jnp.float32)
        d = jnp.zeros((PG_SB, PEER_NK), jnp.float32)
        for j in range(PG_SB):
            d = d + jnp.where(sub == j, dt[:, j * PEER_NK:(j + 1) * PEER_NK], 0.0)
        act = 0.5 * d * (1.0 + lax.erf(d * (2.0 ** -0.5)))
        w = gate_ref[rows, :] * act
        wbd = jnp.where(lane_tok == sub_w, jnp.concatenate([w] * PG_SB, axis=1), 0.0)
        y = x1_ref[rows, :] + jnp.dot(wbd, v, preferred_element_type=jnp.float32)
        ms = jnp.mean(y * y, axis=-1, keepdims=True)
        o_ref[rows, :] = y * lax.rsqrt(ms + NORM_EPS) * lnf_ref[...]


def _peer_gather(ids_flat, gate, h2, x1, lnf_w, uv4, tok0, ntok, after):
    assert tok0 % PG_TB == 0 and ntok % PG_TB == 0
    nsteps = ntok // PG_TB
    b0 = tok0 // PG_TB
    blk = PG_TB * PEER_NK
    tok = lambda w: pl.BlockSpec((PG_TB, w), lambda i: (b0 + i, 0))
    return pl.pallas_call(
        _peer_gather_kernel,
        grid=(nsteps,),
        in_specs=[
            pl.BlockSpec((blk,), lambda i: (b0 + i,), memory_space=pltpu.SMEM),
            pl.BlockSpec((blk,), lambda i: (b0 + jnp.minimum(i + 1, nsteps - 1),), memory_space=pltpu.SMEM),
            tok(PEER_NK), tok(D_MODEL), tok(D_MODEL),
            pl.BlockSpec((1, D_MODEL), lambda i: (0, 0)),
            pl.BlockSpec((SUBLANES, LANES), lambda i: (0, 0)),
            pl.BlockSpec(memory_space=pl.ANY),
        ],
        out_specs=pl.BlockSpec((PG_TB, D_MODEL), lambda i: (i, 0)),
        out_shape=jax.ShapeDtypeStruct((ntok, D_MODEL), jnp.float32),
        scratch_shapes=[pltpu.VMEM((2, PG_NCH, PG_ROWS, LANES), jnp.uint32), pltpu.SemaphoreType.DMA((2,))],
        compiler_params=pltpu.CompilerParams(dimension_semantics=("arbitrary",), vmem_limit_bytes=VMEM_LIMIT),
        name="peer_gather",
    )(ids_flat, ids_flat, gate, h2, x1, lnf_w.reshape(1, D_MODEL), after, uv4)


SC_CORES = 2
SC_SUBCORES = 16
SC_LANES = 16
SC_WORKERS = SC_CORES * SC_SUBCORES
SC_G = 32
SC_NG = PEER_NK // SC_G
SC_NCH = D_MODEL // SC_LANES
SC_HC = SC_NCH // 2


def _sc_pipeline(tab_hbm, ids_hbm, vec_hbm, out_hbm, ids_v, vec_v, rows_v, obuf, gsem, isem, osem, tok_base, tpt, compute):
    wid = lax.axis_index("s") * SC_CORES + lax.axis_index("c")
    tok0 = wid * tpt

    def gather(ts, g):
        return pltpu.make_async_copy(tab_hbm.at[ids_v.at[ts, g]], rows_v.at[g % 2], gsem.at[g % 2])

    def load_inputs(j, ts):
        return (pltpu.make_async_copy(ids_hbm.at[tok_base + j], ids_v.at[ts], isem),
                pltpu.make_async_copy(vec_hbm.at[tok_base + j], vec_v.at[ts], isem))

    def out_copy(j, ts):
        return pltpu.make_async_copy(obuf.at[ts], out_hbm.at[j], osem.at[ts])

    for cp in load_inputs(tok0, 0):
        cp.start()
    for cp in load_inputs(tok0, 0):
        cp.wait()
    gather(0, 0).start()
    gather(0, 1).start()

    @pl.loop(0, tpt, step=2)
    def _(it):
        for ts in range(2):
            j = tok0 + it + ts
            more = (it + ts + 1) < tpt

            @pl.when(more)
            def _():
                for cp in load_inputs(j + 1, 1 - ts):
                    cp.start()

            @pl.when((it + ts) >= 2)
            def _():
                out_copy(j - 2, ts).wait()

            for g in range(SC_NG):
                gather(ts, g).wait()
                compute(ts, g)
                if g + 2 < SC_NG:
                    gather(ts, g + 2).start()
                else:
                    if g + 2 == SC_NG:
                        @pl.when(more)
                        def _():
                            for cp in load_inputs(j + 1, 1 - ts):
                                cp.wait()

                    @pl.when(more)
                    def _():
                        gather(1 - ts, g + 2 - SC_NG).start()
            out_copy(j, ts).start()

    out_copy(tok0 + tpt - 2, 0).wait()
    out_copy(tok0 + tpt - 1, 1).wait()


def _sc_call(body, out_width, vec_width, tok_base, ntok):
    assert ntok % (2 * SC_WORKERS) == 0
    return functools.partial(
        pl.kernel,
        mesh=plsc.VectorSubcoreMesh(core_axis_name="c", subcore_axis_name="s"),
        compiler_params=pltpu.CompilerParams(needs_layout_passes=False),
        out_type=jax.ShapeDtypeStruct((ntok, out_width), jnp.float32),
        scratch_types=[
            pltpu.VMEM((2, SC_NG, SC_G), jnp.int32),
            pltpu.VMEM((2, vec_width), jnp.float32),
            pltpu.VMEM((2, SC_G, D_MODEL), jnp.float32),
            pltpu.VMEM((2, out_width), jnp.float32),
            pltpu.SemaphoreType.DMA((2,)),
            pltpu.SemaphoreType.DMA,
            pltpu.SemaphoreType.DMA((2,)),
        ],
    )(functools.partial(body, tok_base=tok_base, tpt=ntok // SC_WORKERS))


def _sc_dots_body(u_hbm, ids_hbm, h_hbm, out_hbm, ids_v, h_v, rows_v, pbuf, gsem, isem, osem, *, tok_base, tpt):
    def compute(ts, g):
        def chunk(ci, accs):
            hc = h_v[ts, pl.ds(ci * SC_LANES, SC_LANES)]
            return tuple(accs[r] + rows_v[g % 2, r, pl.ds(ci * SC_LANES, SC_LANES)] * hc for r in range(SC_G))
        zeros = tuple(jnp.zeros((SC_LANES,), jnp.float32) for _ in range(SC_G))
        accs = plsc.parallel_loop(0, SC_NCH, carry=zeros)(chunk)
        for r in range(SC_G):
            pbuf[ts, pl.ds((g * SC_G + r) * SC_LANES, SC_LANES)] = accs[r]

    _sc_pipeline(u_hbm, ids_hbm, h_hbm, out_hbm, ids_v, h_v, rows_v, pbuf, gsem, isem, osem, tok_base, tpt, compute)


def _sc_wsum_body(v_hbm, ids_hbm, w_hbm, out_hbm, ids_v, w_v, rows_v, obuf, gsem, isem, osem, *, tok_base, tpt):
    def compute(ts, g):
        for half in range(2):
            def row(r, accs):
                wr = plsc.load_gather(w_v.at[ts], [jnp.full((SC_LANES,), g * SC_G, jnp.int32) + r])
                return tuple(accs[c] + rows_v[g % 2, r, pl.ds((half * SC_HC + c) * SC_LANES, SC_LANES)] * wr
                             for c in range(SC_HC))
            zeros = tuple(jnp.zeros((SC_LANES,), jnp.float32) for _ in range(SC_HC))
            accs = plsc.parallel_loop(0, SC_G, carry=zeros)(row)
            for c in range(SC_HC):
                sl = pl.ds((half * SC_HC + c) * SC_LANES, SC_LANES)
                if g == 0:
                    obuf[ts, sl] = accs[c]
                else:
                    plsc.addupdate(obuf.at[ts, sl], accs[c])

    _sc_pipeline(v_hbm, ids_hbm, w_hbm, out_hbm, ids_v, w_v, rows_v, obuf, gsem, isem, osem, tok_base, tpt, compute)


SCT_TM = 256


def _peer_act_kernel(dp_ref, gate_ref, after_ref, w_ref):
    del after_ref
    lane_exp = lax.broadcasted_iota(jnp.int32, (PEER_NK * SC_LANES, PEER_NK), 0) // SC_LANES
    fold = (lane_exp == lax.broadcasted_iota(jnp.int32, (PEER_NK * SC_LANES, PEER_NK), 1)).astype(jnp.float32)
    d = jnp.dot(dp_ref[...], fold, preferred_element_type=jnp.float32)
    w_ref[...] = gate_ref[...] * (0.5 * d * (1.0 + lax.erf(d * (2.0 ** -0.5))))


def _peer_act(dpart, gate, tok0, after):
    n = dpart.shape[0]
    b0 = tok0 // SCT_TM
    return pl.pallas_call(
        _peer_act_kernel,
        grid=(n // SCT_TM,),
        in_specs=[pl.BlockSpec((SCT_TM, PEER_NK * SC_LANES), lambda i: (i, 0)),
                  pl.BlockSpec((SCT_TM, PEER_NK), lambda i: (b0 + i, 0)),
                  pl.BlockSpec((SUBLANES, LANES), lambda i: (0, 0))],
        out_specs=pl.BlockSpec((SCT_TM, PEER_NK), lambda i: (i, 0)),
        out_shape=jax.ShapeDtypeStruct((n, PEER_NK), jnp.float32),
        compiler_params=pltpu.CompilerParams(dimension_semantics=("arbitrary",)),
        name="peer_act",
    )(dpart, gate, after)


def _peer_final_kernel(x1_ref, p_ref, lnf_ref, o_ref):
    y = x1_ref[...] + p_ref[...]
    o_ref[...] = y * lax.rsqrt(jnp.mean(y * y, axis=-1, keepdims=True) + NORM_EPS) * lnf_ref[...]


def _peer_final(x1, peer, lnf_w, tok0, peer0):
    n = x1.shape[0] - tok0
    assert tok0 % SCT_TM == 0 and peer0 % SCT_TM == 0 and n % SCT_TM == 0
    b0, p0 = tok0 // SCT_TM, peer0 // SCT_TM
    return pl.pallas_call(
        _peer_final_kernel,
        grid=(n // SCT_TM,),
        in_specs=[pl.BlockSpec((SCT_TM, D_MODEL), lambda i: (b0 + i, 0)),
                  pl.BlockSpec((SCT_TM, D_MODEL), lambda i: (p0 + i, 0)),
                  pl.BlockSpec((1, D_MODEL), lambda i: (0, 0))],
        out_specs=pl.BlockSpec((SCT_TM, D_MODEL), lambda i: (i, 0)),
        out_shape=jax.ShapeDtypeStruct((n, D_MODEL), jnp.float32),
        compiler_params=pltpu.CompilerParams(dimension_semantics=("arbitrary",)),
        name="peer_final",
    )(x1, peer, lnf_w.reshape(1, D_MODEL))


PEER_TC_A = 1504
PEER_TC_B = 4896
PEER_SC_PARTS = (2304, 3072, 4608)


def _block(t, mix, route, lnf_w, u_tab, v_tab):
    n_tc = PEER_TC_A + PEER_TC_B
    n_sc = t - n_tc
    routed, dots, off, carry = [], [], 0, None
    assert sum(PEER_SC_PARTS) == n_sc
    for n in PEER_SC_PARTS:
        seg_s, carry = mix(off, n, carry)
        x1_s, h2_s, ids_s, gate_s = route(seg_s, off, 0, n)
        ids_g = ids_s.reshape(n, SC_NG, SC_G)
        dots.append(_sc_call(_sc_dots_body, PEER_NK * SC_LANES, D_MODEL, 0, n)(u_tab, ids_g, h2_s))
        routed.append((x1_s, ids_g, gate_s, off))
        off += n
    seg_t, _ = mix(n_sc, n_tc, carry)
    uv4 = _peer_pack(u_tab, v_tab)
    x1_t, h2_t, ids_t, gate_t = route(seg_t, n_sc, 0, n_tc)
    ids_flat = ids_t.reshape(n_tc * PEER_NK)
    out_a = _peer_gather(ids_flat, gate_t, h2_t, x1_t, lnf_w, uv4, 0, PEER_TC_A, gate_t[:SUBLANES])
    w_sc = jnp.concatenate([_peer_act(d, r[2], 0, out_a[:SUBLANES, :LANES]) for d, r in zip(dots, routed)], axis=0)
    ids_sc = jnp.concatenate([r[1] for r in routed], axis=0)
    peer_sc = _sc_call(_sc_wsum_body, D_MODEL, PEER_NK, 0, n_sc)(v_tab, ids_sc, w_sc)
    out_b = _peer_gather(ids_flat, gate_t, h2_t, x1_t, lnf_w, uv4, PEER_TC_A, PEER_TC_B, w_sc[:SUBLANES])
    outs = [_peer_final(r[0], peer_sc, lnf_w, 0, r[3]) for r in routed]
    return jnp.concatenate(outs + [out_a, out_b], axis=0)


def kernel(x, ln1_w, w_in, ret_norm_w, conv_w, A_log, dt_bias, gdn_norm_w, w_out, ln2_w, peer_wq, peer_keys, peer_u, peer_v, lnf_w):
    b, t, d = x.shape
    assert b == 1 and d == D_MODEL and t % MID_TM == 0 and ln1_w.shape[0] == 1
    l = 0
    x2d = x.reshape(t, d)

    cols = _in_proj_cols()
    w_p = jnp.where((cols >= 0)[None, :], jnp.take(w_in[l], jnp.maximum(cols, 0), axis=1), 0.0).astype(jnp.bfloat16)
    keys_b = peer_keys[l].reshape(2 * PEER_HEADS, PEER_KEYS, PEER_HALF).astype(jnp.bfloat16)
    w_out_b, wq_b = w_out[l].astype(jnp.bfloat16), peer_wq[l].astype(jnp.bfloat16)
    cos4, sin4 = _rope_tables(t)

    def mix(tok0, n, carry):
        if carry is None:
            carry = (jnp.zeros((RET_HEADS, 2 * LANES, RET_V_DIM), jnp.float32),
                     jnp.zeros((GDN_HEADS, GDN_HEAD_DIM, GDN_HEAD_DIM), jnp.float32),
                     jnp.zeros((SUBLANES, CONV_CH), jnp.float32))
        s_ret, s_gdn, tail = carry
        rqk, rv, rg, gqkv, gz, gab = _in_proj(x2d, ln1_w[l], w_p, tok0, n)
        y_r, s_ret = _retention(rqk, rv, rg, cos4, sin4, ret_norm_w[l], s_ret, tok0)
        y_g, s_gdn = _gdn(gqkv, gab, gz, conv_w[l], A_log[l], dt_bias[l], gdn_norm_w[l], s_gdn, tail)
        return (y_r, y_g), (s_ret, s_gdn, gqkv[n - SUBLANES:])

    def route(seg, x0, y0, n):
        x1, h2, st = _mid_proj(x2d, seg[0], seg[1], w_out_b, ln2_w[l], wq_b, keys_b, x0, y0, n)
        ids, gate = _peer_topk(st)
        return x1, h2, ids, gate

    out = _block(t, mix, route, lnf_w, peer_u[l], peer_v[l])
    return out.reshape(b, t, d)
```
